```python
import functools
import jax
import jax.numpy as jnp
from jax import lax
import numpy as np

D_MODEL = 4096
BATCH = 4
SEQ = 2048
DEPTH = 2
DEC_BATCH = 128
DEC_SEQ = 4
PAST_LEN = 16384
PAGE_SIZE = 128

N_EVEN = (DEPTH + 1) // 2
N_ODD = DEPTH // 2
MIX_WIDTH = D_MODEL
GM_GROUPS = 8
GM_WIDTH = MIX_WIDTH // 2
GM_GDIM = GM_WIDTH // GM_GROUPS
GM_CHUNK = 128
MLA_HEADS = 16
MLA_NOPE = 128
MLA_ROPE = 64
MLA_VDIM = (MIX_WIDTH // 2) // MLA_HEADS
MLA_Q_RANK = D_MODEL // 4
MLA_KV_RANK = 512
MLA_SCALE = (MLA_NOPE + MLA_ROPE) ** -0.5
ROPE_THETA = 10000.0
Q_BLOCK = 128
ML_HEADS = 8
ML_DK = (D_MODEL // 2) // ML_HEADS
ML_DV = D_MODEL // ML_HEADS
ML_CHUNK = 64
GATE_CAP = 15.0
FFN_HIDDEN = -(-8 * D_MODEL // (3 * 256)) * 256
EPS = 1e-6

EV_SPLITS = [GM_WIDTH, 2 * GM_WIDTH, 2 * GM_WIDTH + MLA_Q_RANK, 2 * GM_WIDTH + MLA_Q_RANK + MLA_KV_RANK]
EV_IN = EV_SPLITS[-1] + MLA_ROPE
EV_OUT = GM_WIDTH + MLA_HEADS * MLA_VDIM
OD_SPLITS = [ML_HEADS * ML_DK, 2 * ML_HEADS * ML_DK, 2 * ML_HEADS * ML_DK + ML_HEADS * ML_DV,
             2 * ML_HEADS * ML_DK + 2 * ML_HEADS * ML_DV]
OD_IN = OD_SPLITS[-1] + 2 * ML_HEADS

kernel_name = 'hybrid_gmlp_mla_mlstm_adaln_decode_step'


def rms_norm(x, g):
    xf = x.astype(jnp.float32)
    y = xf * lax.rsqrt(jnp.mean(xf * xf, axis=-1, keepdims=True) + EPS)
    return (y * g.astype(jnp.float32)).astype(x.dtype)


def group_layer_norm(v, g, b):
    vf = v.astype(jnp.float32)
    mu = jnp.mean(vf, axis=-1, keepdims=True)
    var = jnp.mean(jnp.square(vf - mu), axis=-1, keepdims=True)
    y = (vf - mu) * lax.rsqrt(var + EPS)
    y = y * g.reshape(GM_GROUPS, GM_GDIM).astype(jnp.float32) + b.reshape(GM_GROUPS, GM_GDIM).astype(jnp.float32)
    return y.astype(v.dtype)


def ada_modulation(c, w, b):
    mod = (jax.nn.silu(c) @ w + b)[:, None, :]
    shift, scale, gate = jnp.split(mod, 3, axis=-1)
    return shift, scale, gate


def modulate(x, g, shift, scale):
    return rms_norm(x, g) * (1.0 + scale) + shift


def rope(x, pos):
    d = x.shape[-1]
    half = d // 2
    inv = ROPE_THETA ** (-jnp.arange(half, dtype=jnp.float32) * (2.0 / d))
    ang = pos.astype(jnp.float32)[:, None] * inv[None, :]
    ang = ang.reshape(ang.shape[:1] + (1,) * (x.ndim - 3) + ang.shape[1:])
    cos, sin = jnp.cos(ang), jnp.sin(ang)
    xf = x.astype(jnp.float32)
    x1, x2 = xf[..., :half], xf[..., half:]
    return jnp.concatenate([x1 * cos - x2 * sin, x2 * cos + x1 * sin], axis=-1).astype(x.dtype)


def mla_attend(q_lat, q_rope, k_lat, k_rope, q_pos, k_pos):
    f32 = jnp.float32
    kl = k_lat.astype(f32)
    s = (jnp.einsum('...thr,...sr->...hts', q_lat.astype(f32), kl)
         + jnp.einsum('...thd,...sd->...hts', q_rope.astype(f32), k_rope.astype(f32))) * MLA_SCALE
    s = jnp.where(k_pos[None, :] <= q_pos[:, None], s, -jnp.inf)
    p = jax.nn.softmax(s, axis=-1)
    return jnp.einsum('...hts,...sr->...thr', p, kl)


def chunk_gmlp(u, v, w_s, b_s):
    bg, t = v.shape[:2]
    L = min(t, GM_CHUNK)
    nc = t // L
    vc = v.reshape(bg, nc, L, GM_GROUPS, GM_GDIM)
    w = jnp.where(jnp.tril(jnp.ones((L, L), dtype=bool)), w_s[:, :L, :L], 0)
    mixed = jnp.einsum('gts,bcsge->bctge', w, vc) + b_s[:, :L].T[:, :, None]
    return u * mixed.reshape(bg, t, GM_GROUPS, GM_GDIM)


def even_mixer(h, pos, attend, w_in, ln_g, ln_b, w_s, b_s, g_qa, w_uq, g_qk, g_kv, g_kr, w_uk, w_uv, w_out):
    bg, t, _ = h.shape
    z = h @ w_in
    u, v, cq, ckv, kr = jnp.split(z, EV_SPLITS, axis=-1)
    u = jax.nn.gelu(u).reshape(bg, t, GM_GROUPS, GM_GDIM)
    v = group_layer_norm(jax.nn.gelu(v).reshape(bg, t, GM_GROUPS, GM_GDIM), ln_g, ln_b)
    q = (rms_norm(cq, g_qa) @ w_uq).reshape(bg, t, MLA_HEADS, MLA_NOPE + MLA_ROPE)
    q = rms_norm(q, g_qk)
    q_nope, q_rope = q[..., :MLA_NOPE], rope(q[..., MLA_NOPE:], pos)
    ckv = rms_norm(ckv, g_kv)
    kr = rope(rms_norm(kr, g_kr), pos)
    q_lat = jnp.einsum('bthn,hnr->bthr', q_nope, w_uk)
    o_lat = attend(q_lat, q_rope, ckv, kr).astype(h.dtype)
    o = jnp.einsum('bthr,hrd->bthd', o_lat, w_uv).reshape(bg, t, MLA_HEADS * MLA_VDIM)
    a = chunk_gmlp(u, v, w_s, b_s).reshape(bg, t, GM_WIDTH)
    out = jnp.concatenate([a, o], axis=-1) @ w_out
    return out, ckv, kr, v.reshape(bg, t, GM_WIDTH)


def mlstm_chunk(carry, inp):
    c0, n0, m0 = carry
    q, k, v, ig, lf = inp
    L = q.shape[1]
    b = jnp.cumsum(lf, axis=1).transpose(0, 2, 1)
    igt = ig.transpose(0, 2, 1)
    causal = jnp.tril(jnp.ones((L, L), dtype=bool))
    dmat = jnp.where(causal, b[..., :, None] - b[..., None, :] + igt[..., None, :], -jnp.inf)
    inter = b + m0[..., None]
    m = jnp.maximum(inter, jnp.max(dmat, axis=-1))
    w = jnp.exp(dmat - m[..., None]) * jnp.einsum('blhd,bshd->bhls', q, k)
    a = jnp.exp(inter - m)
    num = a[..., None] * jnp.einsum('blhd,bhde->bhle', q, c0) + jnp.einsum('bhls,bshe->bhle', w, v)
    nq = a * jnp.einsum('blhd,bhd->bhl', q, n0) + jnp.sum(w, axis=-1)
    den = jnp.maximum(jnp.abs(nq), jnp.exp(-m))
    h = (num / den[..., None]).transpose(0, 2, 1, 3)
    g = b[..., -1:] - b + igt
    m_new = jnp.maximum(b[..., -1] + m0, jnp.max(g, axis=-1))
    decay = jnp.exp(b[..., -1] + m0 - m_new)
    wk = jnp.exp(g - m_new[..., None])
    c_new = decay[..., None, None] * c0 + jnp.einsum('bhs,bshd,bshe->bhde', wk, k, v)
    n_new = decay[..., None] * n0 + jnp.einsum('bhs,bshd->bhd', wk, k)
    return (c_new, n_new, m_new), h


def mlstm_sequence(q, k, v, ig, lf, c0, n0, m0):
    bg, t = q.shape[:2]
    L = min(t, ML_CHUNK)
    nc = t // L

    def to_chunks(arr):
        return jnp.moveaxis(arr.reshape((bg, nc, L) + arr.shape[2:]), 1, 0)

    xs = (to_chunks(q), to_chunks(k), to_chunks(v), to_chunks(ig), to_chunks(lf))
    (c, n, m), h = lax.scan(mlstm_chunk, (c0, n0, m0), xs)
    h = jnp.moveaxis(h, 0, 1).reshape(bg, t, ML_HEADS, ML_DV)
    return h, c, n, m


def odd_mixer(h, w_in, b_gates, g_mh, w_out, c0, n0, m0):
    f32 = jnp.float32
    bg, t, _ = h.shape
    z = h @ w_in
    q, k, v, o, gates = jnp.split(z, OD_SPLITS, axis=-1)
    q = q.reshape(bg, t, ML_HEADS, ML_DK).astype(f32)
    k = k.reshape(bg, t, ML_HEADS, ML_DK).astype(f32) * (ML_DK ** -0.5)
    v = v.reshape(bg, t, ML_HEADS, ML_DV).astype(f32)
    gates = GATE_CAP * jnp.tanh((gates.astype(f32) + b_gates.astype(f32)) / GATE_CAP)
    ig = gates[..., :ML_HEADS]
    lf = jax.nn.log_sigmoid(gates[..., ML_HEADS:])
    hs, c, n, m = mlstm_sequence(q, k, v, ig, lf, c0.astype(f32), n0.astype(f32), m0.astype(f32))
    hs = rms_norm(hs, g_mh.reshape(ML_HEADS, ML_DV)).astype(h.dtype).reshape(bg, t, ML_HEADS * ML_DV)
    out = (jax.nn.sigmoid(o) * hs) @ w_out
    return out, c, n, m


def swiglu(h, w_in, w_out):
    gt, up = jnp.split(h @ w_in, 2, axis=-1)
    return (jax.nn.silu(gt) * up) @ w_out


def setup_inputs(seed: int = 0) -> dict:
    key = jax.random.key(seed)
    ks = iter(jax.random.split(key, 48))
    f32 = jnp.float32

    def nrm(shape, scale=1.0):
        return jax.random.normal(next(ks), shape, f32) * scale

    def gain(shape):
        return 1.0 + 0.02 * nrm(shape)

    n_pages = PAST_LEN // PAGE_SIZE
    n_used = DEC_BATCH * n_pages
    n_phys = n_used + max(1, n_used // 4)
    D, F = D_MODEL, FFN_HIDDEN
    x_prompt = nrm((BATCH, SEQ, D))
    x_sample = nrm((DEC_BATCH, DEC_SEQ, D))
    c_prompt = nrm((BATCH, D))
    c_sample = nrm((DEC_BATCH, D))
    cache_ckv = nrm((N_EVEN, n_phys, PAGE_SIZE, MLA_KV_RANK))
    cache_krope = nrm((N_EVEN, n_phys, PAGE_SIZE, MLA_ROPE))
    page_table = jax.random.permutation(next(ks), n_phys)[:n_used].reshape(DEC_BATCH, n_pages).astype(jnp.int32)
    state_mlstm_C = nrm((N_ODD, DEC_BATCH, ML_HEADS, ML_DK, ML_DV))
    state_mlstm_n = nrm((N_ODD, DEC_BATCH, ML_HEADS, ML_DK))
    state_mlstm_m = nrm((N_ODD, DEC_BATCH, ML_HEADS), 0.5)
    od_b_gates = jnp.concatenate([nrm((N_ODD, ML_HEADS), 0.1), 3.0 + nrm((N_ODD, ML_HEADS), 0.1)], axis=-1)
    return {
        'x_prompt': x_prompt,
        'x_sample': x_sample,
        'c_prompt': c_prompt,
        'c_sample': c_sample,
        'cache_ckv': cache_ckv,
        'cache_krope': cache_krope,
        'page_table': page_table,
        'state_mlstm_C': state_mlstm_C,
        'state_mlstm_n': state_mlstm_n,
        'state_mlstm_m': state_mlstm_m,
        'norm_mix_g': gain((DEPTH, D)),
        'ada_mix_w': nrm((DEPTH, D, 3 * D), 0.5 * D ** -0.5),
        'ada_mix_b': nrm((DEPTH, 3 * D), 0.02),
        'norm_ffn_g': gain((DEPTH, D)),
        'ada_ffn_w': nrm((DEPTH, D, 3 * D), 0.5 * D ** -0.5),
        'ada_ffn_b': nrm((DEPTH, 3 * D), 0.02),
        'ffn_w_in': nrm((DEPTH, D, 2 * F), D ** -0.5),
        'ffn_w_out': nrm((DEPTH, F, D), F ** -0.5),
        'ev_w_in': nrm((N_EVEN, D, EV_IN), D ** -0.5),
        'ev_ln_v_g': gain((N_EVEN, GM_WIDTH)),
        'ev_ln_v_b': nrm((N_EVEN, GM_WIDTH), 0.02),
        'ev_w_s': nrm((N_EVEN, GM_GROUPS, GM_CHUNK, GM_CHUNK), GM_CHUNK ** -0.5),
        'ev_b_s': 1.0 + nrm((N_EVEN, GM_GROUPS, GM_CHUNK), 0.1),
        'ev_g_qa': gain((N_EVEN, MLA_Q_RANK)),
        'ev_w_uq': nrm((N_EVEN, MLA_Q_RANK, MLA_HEADS * (MLA_NOPE + MLA_ROPE)), MLA_Q_RANK ** -0.5),
        'ev_g_qk': gain((N_EVEN, MLA_NOPE + MLA_ROPE)),
        'ev_g_kv': gain((N_EVEN, MLA_KV_RANK)),
        'ev_g_kr': gain((N_EVEN, MLA_ROPE)),
        'ev_w_uk': nrm((N_EVEN, MLA_HEADS, MLA_NOPE, MLA_KV_RANK), MLA_NOPE ** -0.5),
        'ev_w_uv': nrm((N_EVEN, MLA_HEADS, MLA_KV_RANK, MLA_VDIM), MLA_KV_RANK ** -0.5),
        'ev_w_out': nrm((N_EVEN, EV_OUT, D), EV_OUT ** -0.5),
        'od_w_in': nrm((N_ODD, D, OD_IN), D ** -0.5),
        'od_b_gates': od_b_gates,
        'od_g_mh': gain((N_ODD, ML_HEADS * ML_DV)),
        'od_w_out': nrm((N_ODD, ML_HEADS * ML_DV, D), (ML_HEADS * ML_DV) ** -0.5),
    }


def reference(x_prompt, x_sample, c_prompt, c_sample, cache_ckv, cache_krope, page_table,
              state_mlstm_C, state_mlstm_n, state_mlstm_m,
              norm_mix_g, ada_mix_w, ada_mix_b, norm_ffn_g, ada_ffn_w, ada_ffn_b, ffn_w_in, ffn_w_out,
              ev_w_in, ev_ln_v_g, ev_ln_v_b, ev_w_s, ev_b_s, ev_g_qa, ev_w_uq, ev_g_qk, ev_g_kv, ev_g_kr,
              ev_w_uk, ev_w_uv, ev_w_out, od_w_in, od_b_gates, od_g_mh, od_w_out):
    f32 = jnp.float32
    bp, t_p = x_prompt.shape[0], x_prompt.shape[1]
    t_s = x_sample.shape[1]
    past = page_table.shape[1] * PAGE_SIZE
    pos_p = jnp.arange(t_p)
    pos_s = past + jnp.arange(t_s)
    k_pos_s = jnp.arange(past + t_s)

    def prompt_attend(q_lat, q_rope, ckv, kr):
        bg = q_lat.shape[0]

        def block(i):
            start = i * Q_BLOCK
            ql = lax.dynamic_slice_in_dim(q_lat, start, Q_BLOCK, axis=1)
            qr = lax.dynamic_slice_in_dim(q_rope, start, Q_BLOCK, axis=1)
            return mla_attend(ql, qr, ckv, kr, start + jnp.arange(Q_BLOCK), pos_p)

        o = lax.map(block, jnp.arange(t_p // Q_BLOCK))
        return jnp.moveaxis(o, 0, 1).reshape(bg, t_p, MLA_HEADS, MLA_KV_RANK)

    def sample_attend(q_lat, q_rope, ckv, kr, j):
        def one(args):
            pt, ql, qr, c_new, k_new = args
            k_lat = jnp.concatenate([cache_ckv[j, pt].reshape(past, MLA_KV_RANK).astype(f32), c_new.astype(f32)], axis=0)
            k_rot = jnp.concatenate([cache_krope[j, pt].reshape(past, MLA_ROPE).astype(f32), k_new.astype(f32)], axis=0)
            return mla_attend(ql, qr, k_lat, k_rot, pos_s, k_pos_s)

        return lax.map(one, (page_table, q_lat, q_rope, ckv, kr))

    xp, xs = x_prompt, x_sample
    ckv_p, kr_p, ckv_s, kr_s, v_s = [], [], [], [], []
    cp_l, np_l, mp_l, cs_l, ns_l, ms_l = [], [], [], [], [], []
    for l in range(DEPTH):
        j = l // 2
        sh_p, sc_p, gt_p = ada_modulation(c_prompt, ada_mix_w[l], ada_mix_b[l])
        sh_s, sc_s, gt_s = ada_modulation(c_sample, ada_mix_w[l], ada_mix_b[l])
        hp = modulate(xp, norm_mix_g[l], sh_p, sc_p)
        hs = modulate(xs, norm_mix_g[l], sh_s, sc_s)
        if l % 2 == 0:
            ev = (ev_w_in[j], ev_ln_v_g[j], ev_ln_v_b[j], ev_w_s[j], ev_b_s[j], ev_g_qa[j], ev_w_uq[j],
                  ev_g_qk[j], ev_g_kv[j], ev_g_kr[j], ev_w_uk[j], ev_w_uv[j], ev_w_out[j])
            mix_p, ckv_new_p, kr_new_p, _ = even_mixer(hp, pos_p, prompt_attend, *ev)
            mix_s, ckv_new_s, kr_new_s, v_new_s = even_mixer(hs, pos_s, functools.partial(sample_attend, j=j), *ev)
            ckv_p.append(ckv_new_p)
            kr_p.append(kr_new_p)
            ckv_s.append(ckv_new_s)
            kr_s.append(kr_new_s)
            v_s.append(v_new_s)
        else:
            zc = jnp.zeros((bp, ML_HEADS, ML_DK, ML_DV), f32)
            zn = jnp.zeros((bp, ML_HEADS, ML_DK), f32)
            zm = jnp.zeros((bp, ML_HEADS), f32)
            mix_p, c_p, n_p, m_p = odd_mixer(hp, od_w_in[j], od_b_gates[j], od_g_mh[j], od_w_out[j], zc, zn, zm)
            mix_s, c_s, n_s, m_s = odd_mixer(hs, od_w_in[j], od_b_gates[j], od_g_mh[j], od_w_out[j],
                                             state_mlstm_C[j], state_mlstm_n[j], state_mlstm_m[j])
            cp_l.append(c_p)
            np_l.append(n_p)
            mp_l.append(m_p)
            cs_l.append(c_s)
            ns_l.append(n_s)
            ms_l.append(m_s)
        xp = xp + gt_p * mix_p
        xs = xs + gt_s * mix_s
        sh_p, sc_p, gt_p = ada_modulation(c_prompt, ada_ffn_w[l], ada_ffn_b[l])
        sh_s, sc_s, gt_s = ada_modulation(c_sample, ada_ffn_w[l], ada_ffn_b[l])
        xp = xp + gt_p * swiglu(modulate(xp, norm_ffn_g[l], sh_p, sc_p), ffn_w_in[l], ffn_w_out[l])
        xs = xs + gt_s * swiglu(modulate(xs, norm_ffn_g[l], sh_s, sc_s), ffn_w_in[l], ffn_w_out[l])

    return (xp, xs,
            jnp.stack(ckv_p), jnp.stack(kr_p), jnp.stack(ckv_s), jnp.stack(kr_s), jnp.stack(v_s),
            jnp.stack(cp_l), jnp.stack(np_l), jnp.stack(mp_l),
            jnp.stack(cs_l), jnp.stack(ns_l), jnp.stack(ms_l))
```

```python
import functools

import jax
import jax.numpy as jnp
from jax import lax
from jax.experimental import pallas as pl
from jax.experimental.pallas import tpu as pltpu

EPS = 1e-6
ROPE_THETA = 10000.0
GATE_CAP = 15.0
ML_CHUNK = 64
ML_SAMPLE_PAD = 16
PAGES_PER_STEP = 16
LANES = 128
V7X_VMEM_LIMIT = 56 * 1024 * 1024

F32 = jnp.float32
BF16 = jnp.bfloat16


def _cparams(*sem):
    return pltpu.CompilerParams(dimension_semantics=sem, vmem_limit_bytes=V7X_VMEM_LIMIT)


def _div_tile(dim, pref, align):
    if dim <= pref:
        return dim
    t = (pref // align) * align
    while t >= align:
        if dim % t == 0:
            return t
        t -= align
    return dim


def _rms(x, g):
    return x * lax.rsqrt(jnp.mean(x * x, axis=-1, keepdims=True) + EPS) * g


def _gelu(x):
    return x * (0.5 * (1.0 + jnp.tanh(0.7978845608028654 * (x + 0.044715 * (x * x * x)))))


def _sigmoid(x):
    return 1.0 / (1.0 + jnp.exp(-x))


def _dot(a, b):
    return jnp.dot(a, b, preferred_element_type=F32)


def _dot_nt(a, b):
    return lax.dot_general(a, b, (((1,), (1,)), ((), ())), preferred_element_type=F32)


def _dot_tn(a, b):
    return lax.dot_general(a, b, (((0,), (0,)), ((), ())), preferred_element_type=F32)


def _mm_kernel(*refs, nk, nx, nka, silu_x, has_bias, resid):
    it = iter(refs)
    x_refs = [next(it) for _ in range(nx)]
    w_ref = next(it)
    b_ref = next(it) if has_bias else None
    res_ref = next(it) if resid else None
    gate_ref = next(it) if resid else None
    o_ref = next(it)
    acc_ref = next(it)
    k = pl.program_id(2)

    @pl.when(k == 0)
    def _zero():
        acc_ref[...] = jnp.zeros_like(acc_ref)

    def accum(x_ref):
        x = x_ref[...]
        if silu_x:
            x = x * _sigmoid(x)
        acc_ref[...] += _dot(x.astype(BF16), w_ref[...].astype(BF16))

    if nx == 1:
        accum(x_refs[0])
    else:
        pl.when(k < nka)(lambda: accum(x_refs[0]))
        pl.when(k >= nka)(lambda: accum(x_refs[1]))

    @pl.when(k == nk - 1)
    def _finish():
        y = acc_ref[...]
        if has_bias:
            y = y + b_ref[...]
        if resid:
            y = res_ref[...] + gate_ref[...] * y
        o_ref[...] = y.astype(o_ref.dtype)


def _matmul(xs, w, *, layer, n_out, tm, tn, tk, out_dtype, bias=None, silu_x=False,
            res=None, gate=None, gate_group_rows=None, name="mm"):
    m = xs[0].shape[0]
    k_each = xs[0].shape[1]
    k_total = sum(x.shape[1] for x in xs)
    assert w.shape[1] == k_total and all(x.shape[1] == k_each for x in xs)
    assert m % tm == 0 and k_each % tk == 0
    nka = k_each // tk
    nk = k_total // tk
    nx = len(xs)
    grid = (m // tm, pl.cdiv(n_out, tn), nk)

    in_specs = [pl.BlockSpec((tm, tk), lambda i, j, k: (i, jnp.minimum(k, nka - 1)))]
    if nx == 2:
        in_specs.append(pl.BlockSpec((tm, tk), lambda i, j, k: (i, jnp.maximum(k - nka, 0))))
    in_specs.append(pl.BlockSpec((None, tk, tn), lambda i, j, k: (layer, k, j)))
    args = list(xs) + [w]
    if bias is not None:
        in_specs.append(pl.BlockSpec((None, 1, tn), lambda i, j, k: (layer, 0, j)))
        args.append(bias)
    if res is not None:
        in_specs.append(pl.BlockSpec((tm, tn), lambda i, j, k: (i, j)))
        args.append(res)
        if gate.ndim == 3:
            assert gate_group_rows % tm == 0
            per = gate_group_rows // tm
            in_specs.append(pl.BlockSpec((None, 1, tn), lambda i, j, k: (i // per, 0, j)))
        else:
            in_specs.append(pl.BlockSpec((tm, tn), lambda i, j, k: (i, j)))
        args.append(gate)
    kern = functools.partial(_mm_kernel, nk=nk, nx=nx, nka=nka, silu_x=silu_x,
                             has_bias=bias is not None, resid=res is not None)
    return pl.pallas_call(
        kern,
        grid=grid,
        in_specs=in_specs,
        out_specs=pl.BlockSpec((tm, tn), lambda i, j, k: (i, j)),
        out_shape=jax.ShapeDtypeStruct((m, n_out), out_dtype),
        scratch_shapes=[pltpu.VMEM((tm, tn), F32)],
        compiler_params=_cparams("parallel", "parallel", "arbitrary"),
        name=name,
    )(*args)


SWIGLU_SUB = 256


def _swiglu_kernel(x_ref, *refs, nk, nsub):
    wg = refs[:nsub]
    wu = refs[nsub:2 * nsub]
    o_ref = refs[2 * nsub]
    accg, accu = refs[2 * nsub + 1:]
    k = pl.program_id(2)

    @pl.when(k == 0)
    def _zero():
        accg[...] = jnp.zeros_like(accg)
        accu[...] = jnp.zeros_like(accu)

    x = x_ref[...]
    for s in range(nsub):
        cols = slice(s * SWIGLU_SUB, (s + 1) * SWIGLU_SUB)
        accg[:, cols] += _dot(x, wg[s][...].astype(BF16))
        accu[:, cols] += _dot(x, wu[s][...].astype(BF16))

    @pl.when(k == nk - 1)
    def _finish():
        g = accg[...]
        o_ref[...] = (g * _sigmoid(g) * accu[...]).astype(o_ref.dtype)


def _swiglu_in(x, w, *, layer, tm, tk, nsub):
    m, kdim = x.shape
    f = w.shape[2] // 2
    sub = SWIGLU_SUB
    assert f % sub == 0 and m % tm == 0 and kdim % tk == 0
    nfb = f // sub
    last = 2 * nfb - 1
    nk = kdim // tk
    tn = nsub * sub
    grid = (m // tm, pl.cdiv(f, tn), nk)
    in_specs = [pl.BlockSpec((tm, tk), lambda i, j, k: (i, k))]
    for s in range(nsub):
        in_specs.append(pl.BlockSpec((None, tk, sub), lambda i, j, k, s=s: (layer, k, j * nsub + s)))
    for s in range(nsub):
        in_specs.append(pl.BlockSpec(
            (None, tk, sub), lambda i, j, k, s=s: (layer, k, jnp.minimum(nfb + j * nsub + s, last))))
    return pl.pallas_call(
        functools.partial(_swiglu_kernel, nk=nk, nsub=nsub),
        grid=grid,
        in_specs=in_specs,
        out_specs=pl.BlockSpec((tm, tn), lambda i, j, k: (i, j)),
        out_shape=jax.ShapeDtypeStruct((m, f), BF16),
        scratch_shapes=[pltpu.VMEM((tm, tn), F32), pltpu.VMEM((tm, tn), F32)],
        compiler_params=_cparams("parallel", "parallel", "arbitrary"),
        name="swiglu_in",
    )(x, *([w] * (2 * nsub)))


def _modulate_kernel(x_ref, g_ref, shift_ref, scale_ref, o_ref):
    y = _rms(x_ref[...], g_ref[...])
    o_ref[...] = (y * (1.0 + scale_ref[...]) + shift_ref[...]).astype(o_ref.dtype)


def _modulate(x, g, shift, scale, *, layer, tr):
    gg, r, d = x.shape
    rm = shift.shape[1]
    mod_rows = 1 if rm == 1 else tr
    mod_map = (lambda a, b: (a, 0, 0)) if rm == 1 else (lambda a, b: (a, b, 0))
    out = pl.pallas_call(
        _modulate_kernel,
        grid=(gg, r // tr),
        in_specs=[
            pl.BlockSpec((None, tr, d), lambda a, b: (a, b, 0)),
            pl.BlockSpec((None, 1, d), lambda a, b: (layer, 0, 0)),
            pl.BlockSpec((None, mod_rows, d), mod_map),
            pl.BlockSpec((None, mod_rows, d), mod_map),
        ],
        out_specs=pl.BlockSpec((None, tr, d), lambda a, b: (a, b, 0)),
        out_shape=jax.ShapeDtypeStruct((gg, r, d), BF16),
        compiler_params=_cparams("parallel", "parallel"),
        name="modulate",
    )(x, g, shift, scale)
    return out.reshape(gg * r, d)


def _gmlp_kernel(u_ref, v_ref, lng_ref, lnb_ref, w_ref, b_ref, a_ref, *vout, groups, gdim):
    for g in range(groups):
        cols = slice(g * gdim, (g + 1) * gdim)
        vg = _gelu(v_ref[:, cols])
        mu = jnp.mean(vg, axis=-1, keepdims=True)
        xc = vg - mu
        var = jnp.mean(xc * xc, axis=-1, keepdims=True)
        y = xc * lax.rsqrt(var + EPS) * lng_ref[:, cols] + lnb_ref[:, cols]
        if vout:
            vout[0][:, cols] = y
        mixed = _dot(w_ref[g].astype(BF16), y.astype(BF16)) + b_ref[:, g:g + 1]
        a_ref[:, cols] = (_gelu(u_ref[:, cols]) * mixed).astype(a_ref.dtype)


def _gmlp(z, ln_g, ln_b, w_mix, b_mix, *, layer, gw, emit_v):
    m = z.shape[0]
    groups = w_mix.shape[0]
    rows = w_mix.shape[1]
    out_shape = [jax.ShapeDtypeStruct((m, gw), BF16)]
    out_specs = [pl.BlockSpec((rows, gw), lambda i: (i, 0))]
    if emit_v:
        out_shape.append(jax.ShapeDtypeStruct((m, gw), F32))
        out_specs.append(pl.BlockSpec((rows, gw), lambda i: (i, 0)))
    res = pl.pallas_call(
        functools.partial(_gmlp_kernel, groups=groups, gdim=gw // groups),
        grid=(m // rows,),
        in_specs=[
            pl.BlockSpec((rows, gw), lambda i: (i, 0)),
            pl.BlockSpec((rows, gw), lambda i: (i, 1)),
            pl.BlockSpec((None, 1, gw), lambda i: (layer, 0, 0)),
            pl.BlockSpec((None, 1, gw), lambda i: (layer, 0, 0)),
            pl.BlockSpec((groups, rows, rows), lambda i: (0, 0, 0)),
            pl.BlockSpec((rows, groups), lambda i: (0, 0)),
        ],
        out_specs=out_specs,
        out_shape=out_shape,
        compiler_params=_cparams("parallel"),
        name="gmlp",
    )(z, z, ln_g, ln_b, w_mix, b_mix)
    return res if emit_v else (res[0], None)


def _rope_rows(x, cosf, sinf):
    half = x.shape[-1] // 2
    swapped = jnp.concatenate([x[:, half:], x[:, :half]], axis=-1)
    return x * cosf + swapped * sinf


def _q_kernel(cq_ref, gqa_ref, wuq_ref, gqk_ref, cos_ref, sin_ref, wuk_ref, o_ref, cqn_sc,
              *, nope, rank, scale):
    @pl.when(pl.program_id(1) == 0)
    def _norm():
        cqn_sc[...] = _rms(cq_ref[...], gqa_ref[...]).astype(BF16)

    q = _dot(cqn_sc[...], wuq_ref[...].astype(BF16))
    q = _rms(q, gqk_ref[...])
    q_rope = _rope_rows(q[:, nope:], cos_ref[...], sin_ref[...])
    q_lat = _dot(q[:, :nope].astype(BF16), wuk_ref[...].astype(BF16))
    o_ref[:, :rank] = (q_lat * scale).astype(o_ref.dtype)
    o_ref[:, rank:] = (q_rope * scale).astype(o_ref.dtype)


def _mla_q(z, g_qa, w_uq_h, g_qk, cosf, sinf, w_uk, *, layer, cq_off, tm, scale):
    m = z.shape[0]
    qr = g_qa.shape[-1]
    heads, _, qd = w_uq_h.shape
    nope, rank = w_uk.shape[2], w_uk.shape[3]
    rope_d = qd - nope
    assert cq_off % qr == 0
    cq_blk = cq_off // qr
    return pl.pallas_call(
        functools.partial(_q_kernel, nope=nope, rank=rank, scale=scale),
        grid=(m // tm, heads),
        in_specs=[
            pl.BlockSpec((tm, qr), lambda i, h: (i, cq_blk)),
            pl.BlockSpec((None, 1, qr), lambda i, h: (layer, 0, 0)),
            pl.BlockSpec((None, qr, qd), lambda i, h: (h, 0, 0)),
            pl.BlockSpec((None, 1, qd), lambda i, h: (layer, 0, 0)),
            pl.BlockSpec((tm, rope_d), lambda i, h: (i, 0)),
            pl.BlockSpec((tm, rope_d), lambda i, h: (i, 0)),
            pl.BlockSpec((None, None, nope, rank), lambda i, h: (layer, h, 0, 0)),
        ],
        out_specs=pl.BlockSpec((None, tm, rank + rope_d), lambda i, h: (h, i, 0)),
        out_shape=jax.ShapeDtypeStruct((heads, m, rank + rope_d), BF16),
        scratch_shapes=[pltpu.VMEM((tm, qr), BF16)],
        compiler_params=_cparams("parallel", "arbitrary"),
        name="mla_q",
    )(z, g_qa, w_uq_h, g_qk, cosf, sinf, w_uk)


def _kv_kernel(ckv_ref, kr_ref, gkv_ref, gkr_ref, cos_ref, sin_ref, ckv_o, kr_o, kcat_o,
               *, rank, rope_d):
    c = _rms(ckv_ref[...], gkv_ref[...])
    r = _rope_rows(_rms(kr_ref[:, :rope_d], gkr_ref[...]), cos_ref[...], sin_ref[...])
    ckv_o[...] = c
    kr_o[...] = r
    kcat_o[:, :rank] = c.astype(kcat_o.dtype)
    kcat_o[:, rank:] = r.astype(kcat_o.dtype)


def _mla_kv(z, g_kv, g_kr, cosf, sinf, *, layer, ckv_off, kr_off, tm):
    m = z.shape[0]
    rank = g_kv.shape[-1]
    rope_d = g_kr.shape[-1]
    assert ckv_off % rank == 0 and kr_off % LANES == 0
    return pl.pallas_call(
        functools.partial(_kv_kernel, rank=rank, rope_d=rope_d),
        grid=(m // tm,),
        in_specs=[
            pl.BlockSpec((tm, rank), lambda i: (i, ckv_off // rank)),
            pl.BlockSpec((tm, LANES), lambda i: (i, kr_off // LANES)),
            pl.BlockSpec((None, 1, rank), lambda i: (layer, 0, 0)),
            pl.BlockSpec((None, 1, rope_d), lambda i: (layer, 0, 0)),
            pl.BlockSpec((tm, rope_d), lambda i: (i, 0)),
            pl.BlockSpec((tm, rope_d), lambda i: (i, 0)),
        ],
        out_specs=[
            pl.BlockSpec((tm, rank), lambda i: (i, 0)),
            pl.BlockSpec((tm, rope_d), lambda i: (i, 0)),
            pl.BlockSpec((tm, rank + rope_d), lambda i: (i, 0)),
        ],
        out_shape=[
            jax.ShapeDtypeStruct((m, rank), F32),
            jax.ShapeDtypeStruct((m, rope_d), F32),
            jax.ShapeDtypeStruct((m, rank + rope_d), BF16),
        ],
        compiler_params=_cparams("parallel"),
        name="mla_kv",
    )(z, z, g_kv, g_kr, cosf, sinf)


def _pattn_kernel(q_ref, k_ref, wuv_ref, o_ref, acc_sc, *, tq, rank):
    qi = pl.program_id(2)
    q = q_ref[...]

    def step(kb, carry, masked):
        m_prev, l_prev = carry
        kblk = k_ref[pl.ds(pl.multiple_of(kb * tq, tq), tq), :]
        s = _dot_nt(q, kblk)
        if masked:
            row = lax.broadcasted_iota(jnp.int32, s.shape, 0)
            col = lax.broadcasted_iota(jnp.int32, s.shape, 1)
            s = jnp.where(col <= row, s, -jnp.inf)
        m_new = jnp.maximum(m_prev, jnp.max(s, axis=-1, keepdims=True))
        alpha = jnp.exp(m_prev - m_new)
        p = jnp.exp(s - m_new)
        l_new = alpha * l_prev + jnp.sum(p, axis=-1, keepdims=True)
        acc_sc[...] = alpha * acc_sc[...] + _dot(p.astype(BF16), kblk[:, :rank])
        return m_new, l_new

    acc_sc[...] = jnp.zeros_like(acc_sc)
    init = (jnp.full((tq, 1), -jnp.inf, F32), jnp.zeros((tq, 1), F32))
    carry = lax.fori_loop(0, qi, lambda kb, c: step(kb, c, False), init)
    _, l_fin = step(qi, carry, True)
    o_lat = acc_sc[...] / l_fin
    o_ref[...] = _dot(o_lat.astype(BF16), wuv_ref[...].astype(BF16)).astype(o_ref.dtype)


def _prompt_attention(qcat, kcat, w_uv, *, layer, tq):
    heads, b, t, qd = qcat.shape
    rank, vd = w_uv.shape[2], w_uv.shape[3]
    nq = t // tq
    return pl.pallas_call(
        functools.partial(_pattn_kernel, tq=tq, rank=rank),
        grid=(b, heads, nq),
        in_specs=[
            pl.BlockSpec((None, None, tq, qd), lambda bi, h, qi: (h, bi, qi, 0)),
            pl.BlockSpec((None, t, qd), lambda bi, h, qi: (bi, 0, 0)),
            pl.BlockSpec((None, None, rank, vd), lambda bi, h, qi: (layer, h, 0, 0)),
        ],
        out_specs=pl.BlockSpec((tq, vd), lambda bi, h, qi: (bi * nq + qi, h)),
        out_shape=jax.ShapeDtypeStruct((b * t, heads * vd), BF16),
        scratch_shapes=[pltpu.VMEM((tq, rank), F32)],
        compiler_params=_cparams("parallel", "parallel", "arbitrary"),
        name="prompt_attn",
    )(qcat, kcat, w_uv)


def _sattn_kernel(pt_ref, q_ref, knew_ref, *refs, pb, rank, nsteps, t_new):
    del pt_ref
    ckv_refs = refs[:pb]
    kr_refs = refs[pb:2 * pb]
    o_ref = refs[2 * pb]
    m_sc, l_sc, acc_sc = refs[2 * pb + 1:]
    step = pl.program_id(1)

    @pl.when(step == 0)
    def _init():
        m_sc[...] = jnp.full_like(m_sc, -jnp.inf)
        l_sc[...] = jnp.zeros_like(l_sc)
        acc_sc[...] = jnp.zeros_like(acc_sc)

    q = q_ref[...]
    q_lat = q[:, :rank]
    q_rope = q[:, rank:]

    def online_update(scores, values):
        m_prev = m_sc[...]
        m_new = m_prev
        for s in scores:
            m_new = jnp.maximum(m_new, jnp.max(s, axis=-1, keepdims=True))
        alpha = jnp.exp(m_prev - m_new)
        l_new = alpha * l_sc[...]
        acc = alpha * acc_sc[...]
        for s, v in zip(scores, values):
            p = jnp.exp(s - m_new)
            l_new = l_new + jnp.sum(p, axis=-1, keepdims=True)
            acc = acc + _dot(p.astype(BF16), v)
        m_sc[...] = m_new
        l_sc[...] = l_new
        acc_sc[...] = acc

    scores, values = [], []
    for i in range(pb):
        k_lat = ckv_refs[i][...].astype(BF16)
        k_rope = kr_refs[i][...].astype(BF16)
        scores.append(_dot_nt(q_lat, k_lat) + _dot_nt(q_rope, k_rope))
        values.append(k_lat)
    online_update(scores, values)

    @pl.when(step == nsteps - 1)
    def _finish():
        k_new = knew_ref[...]
        s = _dot_nt(q, k_new)
        row = lax.broadcasted_iota(jnp.int32, s.shape, 0)
        col = lax.broadcasted_iota(jnp.int32, s.shape, 1)
        s = jnp.where(col <= row % t_new, s, -jnp.inf)
        online_update([s], [k_new[:, :rank]])
        o_ref[...] = (acc_sc[...] / l_sc[...]).astype(o_ref.dtype)


def _sample_attention(q_s, k_new, cache_ckv, cache_krope, page_table, *, layer, t_new):
    b, rows, qd = q_s.shape
    page, rank = cache_ckv.shape[2], cache_ckv.shape[3]
    rope_d = cache_krope.shape[3]
    n_pages = page_table.shape[1]
    pb = _div_tile(n_pages, PAGES_PER_STEP, 1)
    nsteps = n_pages // pb
    in_specs = [
        pl.BlockSpec((None, rows, qd), lambda bi, s, pt: (bi, 0, 0)),
        pl.BlockSpec((None, k_new.shape[1], qd), lambda bi, s, pt: (bi, 0, 0)),
    ]
    for i in range(pb):
        in_specs.append(pl.BlockSpec(
            (None, None, page, rank),
            lambda bi, s, pt, i=i: (layer, pt[bi * n_pages + s * pb + i], 0, 0)))
    for i in range(pb):
        in_specs.append(pl.BlockSpec(
            (None, None, page, rope_d),
            lambda bi, s, pt, i=i: (layer, pt[bi * n_pages + s * pb + i], 0, 0)))
    grid_spec = pltpu.PrefetchScalarGridSpec(
        num_scalar_prefetch=1,
        grid=(b, nsteps),
        in_specs=in_specs,
        out_specs=pl.BlockSpec((None, rows, rank), lambda bi, s, pt: (bi, 0, 0)),
        scratch_shapes=[pltpu.VMEM((rows, 1), F32), pltpu.VMEM((rows, 1), F32),
                        pltpu.VMEM((rows, rank), F32)],
    )
    return pl.pallas_call(
        functools.partial(_sattn_kernel, pb=pb, rank=rank, nsteps=nsteps, t_new=t_new),
        grid_spec=grid_spec,
        out_shape=jax.ShapeDtypeStruct((b, rows, rank), BF16),
        compiler_params=_cparams("parallel", "arbitrary"),
        name="sample_attn",
    )(page_table.reshape(-1), q_s, k_new, *([cache_ckv] * pb), *([cache_krope] * pb))


def _uv_kernel(x_ref, w_ref, o_ref):
    o_ref[...] = _dot(x_ref[...], w_ref[...].astype(BF16)).astype(o_ref.dtype)


def _value_up(o_lat, w_uv, *, layer):
    heads, m, rank = o_lat.shape
    vd = w_uv.shape[3]
    return pl.pallas_call(
        _uv_kernel,
        grid=(heads,),
        in_specs=[
            pl.BlockSpec((None, m, rank), lambda h: (h, 0, 0)),
            pl.BlockSpec((None, None, rank, vd), lambda h: (layer, h, 0, 0)),
        ],
        out_specs=pl.BlockSpec((m, vd), lambda h: (0, h)),
        out_shape=jax.ShapeDtypeStruct((m, heads * vd), BF16),
        compiler_params=_cparams("parallel"),
        name="value_up",
    )(o_lat, w_uv)


def _mlstm_kernel(q_ref, k_ref, v_ref, o_ref, gc_ref, gr_ref, bgc_ref, bgr_ref, gmh_ref,
                  c0_ref, n0_ref, m0_ref, h_ref, c_ref, n_ref, m_ref,
                  *, heads, dk, dv, rows, valid):
    @pl.when(pl.program_id(1) == 0)
    def _load_state():
        c_ref[...] = c0_ref[...]
        n_ref[...] = n0_ref[...]
        m_ref[...] = m0_ref[...]

    g_col = GATE_CAP * jnp.tanh((gc_ref[:, :2 * heads] + bgc_ref[...]) / GATE_CAP)
    g_row = GATE_CAP * jnp.tanh((gr_ref[...] + bgr_ref[...]) / GATE_CAP)

    def log_sigmoid(x):
        return jnp.minimum(x, 0.0) - jnp.log(1.0 + jnp.exp(-jnp.abs(x)))

    li = lax.broadcasted_iota(jnp.int32, (rows, rows), 0)
    si = lax.broadcasted_iota(jnp.int32, (rows, rows), 1)
    causal = si <= li
    row_ok = lax.broadcasted_iota(jnp.int32, (rows, 1), 0) < valid
    col_ok = lax.broadcasted_iota(jnp.int32, (1, rows), 1) < valid
    k_scale = dk ** -0.5

    for h in range(heads):
        ig_c = jnp.where(row_ok, g_col[:, h:h + 1], -jnp.inf)
        lf_c = jnp.where(row_ok, log_sigmoid(g_col[:, heads + h:heads + h + 1]), 0.0)
        ig_r = jnp.where(col_ok, g_row[h:h + 1, :], -jnp.inf)
        lf_r = jnp.where(col_ok, log_sigmoid(g_row[heads + h:heads + h + 1, :]), 0.0)
        b_c = jnp.sum(jnp.where(causal, lf_r, 0.0), axis=1, keepdims=True)
        b_r = jnp.sum(jnp.where(li <= si, lf_c, 0.0), axis=0, keepdims=True)
        b_last = jnp.sum(lf_r, axis=1, keepdims=True)
        m0 = m_ref[h:h + 1, :]
        n0 = n_ref[h:h + 1, :]
        c0 = c_ref[h]

        qh = q_ref[:, h * dk:(h + 1) * dk]
        kh = k_ref[:, h * dk:(h + 1) * dk] * k_scale
        vh = v_ref[:, h * dv:(h + 1) * dv]
        qb = qh.astype(BF16)
        kb = kh.astype(BF16)
        vb = vh.astype(BF16)

        dmat = jnp.where(causal, b_c - b_r + ig_r, -jnp.inf)
        inter = b_c + m0
        m_pos = jnp.maximum(inter, jnp.max(dmat, axis=1, keepdims=True))
        w = jnp.exp(dmat - m_pos) * _dot_nt(qb, kb)
        a = jnp.exp(inter - m_pos)
        num = a * _dot(qb, c0.astype(BF16)) + _dot(w.astype(BF16), vb)
        nq = a * jnp.sum(qh * n0, axis=1, keepdims=True) + jnp.sum(w, axis=1, keepdims=True)
        den = jnp.maximum(jnp.abs(nq), jnp.exp(-m_pos))
        hs = num / den

        g_r = b_last - b_r + ig_r
        m_new = jnp.maximum(b_last + m0, jnp.max(g_r, axis=1, keepdims=True))
        decay = jnp.exp(b_last + m0 - m_new)
        wk_c = jnp.exp(b_last - b_c + ig_c - m_new)
        kw = wk_c * kh
        c_ref[h] = decay * c0 + _dot_tn(kw.astype(BF16), vb)
        n_ref[h:h + 1, :] = decay * n0 + jnp.sum(kw, axis=0, keepdims=True)
        m_ref[h:h + 1, :] = m_new

        hn = _rms(hs, gmh_ref[:, h * dv:(h + 1) * dv])
        og = _sigmoid(o_ref[:, h * dv:(h + 1) * dv])
        h_ref[:, h * dv:(h + 1) * dv] = (og * hn).astype(h_ref.dtype)


def _mlstm(z3, g_rows, b_gates_c, b_gates_r, g_mh, c0, n0, m0, *, layer, nb, nc, valid):
    _, rows, _ = z3.shape
    _, heads, dk, dv = c0.shape
    qw, vw = heads * dk, heads * dv
    assert (2 * qw) % vw == 0 and (2 * qw + 2 * vw) % LANES == 0
    g_blk = (2 * qw + 2 * vw) // LANES
    seq = lambda b, c: b * nc + c
    outs = pl.pallas_call(
        functools.partial(_mlstm_kernel, heads=heads, dk=dk, dv=dv, rows=rows, valid=valid),
        grid=(nb, nc),
        in_specs=[
            pl.BlockSpec((None, rows, qw), lambda b, c: (seq(b, c), 0, 0)),
            pl.BlockSpec((None, rows, qw), lambda b, c: (seq(b, c), 0, 1)),
            pl.BlockSpec((None, rows, vw), lambda b, c: (seq(b, c), 0, 2 * qw // vw)),
            pl.BlockSpec((None, rows, vw), lambda b, c: (seq(b, c), 0, 2 * qw // vw + 1)),
            pl.BlockSpec((None, rows, LANES), lambda b, c: (seq(b, c), 0, g_blk)),
            pl.BlockSpec((None, 2 * heads, rows), lambda b, c: (seq(b, c), 0, 0)),
            pl.BlockSpec((None, 1, 2 * heads), lambda b, c: (layer, 0, 0)),
            pl.BlockSpec((None, 2 * heads, 1), lambda b, c: (layer, 0, 0)),
            pl.BlockSpec((None, 1, vw), lambda b, c: (layer, 0, 0)),
            pl.BlockSpec((None, heads, dk, dv), lambda b, c: (b, 0, 0, 0)),
            pl.BlockSpec((None, heads, dk), lambda b, c: (b, 0, 0)),
            pl.BlockSpec((None, heads, 1), lambda b, c: (b, 0, 0)),
        ],
        out_specs=[
            pl.BlockSpec((None, rows, vw), lambda b, c: (seq(b, c), 0, 0)),
            pl.BlockSpec((None, heads, dk, dv), lambda b, c: (b, 0, 0, 0)),
            pl.BlockSpec((None, heads, dk), lambda b, c: (b, 0, 0)),
            pl.BlockSpec((None, heads, 1), lambda b, c: (b, 0, 0)),
        ],
        out_shape=[
            jax.ShapeDtypeStruct((nb * nc, rows, vw), BF16),
            jax.ShapeDtypeStruct((nb, heads, dk, dv), F32),
            jax.ShapeDtypeStruct((nb, heads, dk), F32),
            jax.ShapeDtypeStruct((nb, heads, 1), F32),
        ],
        compiler_params=_cparams("parallel", "arbitrary"),
        name="mlstm",
    )(z3, z3, z3, z3, z3, g_rows, b_gates_c, b_gates_r, g_mh, c0, n0, m0)
    return outs


def _rope_tables(pos, rope_d):
    half = rope_d // 2
    inv = ROPE_THETA ** (-jnp.arange(half, dtype=F32) * (2.0 / rope_d))
    ang = pos.astype(F32)[:, None] * inv[None, :]
    cos, sin = jnp.cos(ang), jnp.sin(ang)
    return jnp.concatenate([cos, cos], axis=-1), jnp.concatenate([-sin, sin], axis=-1)


def kernel(x_prompt, x_sample, c_prompt, c_sample, cache_ckv, cache_krope, page_table,
           state_mlstm_C, state_mlstm_n, state_mlstm_m,
           norm_mix_g, ada_mix_w, ada_mix_b, norm_ffn_g, ada_ffn_w, ada_ffn_b, ffn_w_in, ffn_w_out,
           ev_w_in, ev_ln_v_g, ev_ln_v_b, ev_w_s, ev_b_s, ev_g_qa, ev_w_uq, ev_g_qk, ev_g_kv, ev_g_kr,
           ev_w_uk, ev_w_uv, ev_w_out, od_w_in, od_b_gates, od_g_mh, od_w_out):
    bp, tp, d = x_prompt.shape
    bs, ts, _ = x_sample.shape
    depth = norm_mix_g.shape[0]
    mp, ms = bp * tp, bs * ts
    past = page_table.shape[1] * cache_ckv.shape[2]

    n_cond = bp + bs
    cond_rows = -(-n_cond // 16) * 16
    c_all = jnp.concatenate([c_prompt, c_sample, jnp.zeros((cond_rows - n_cond, d), F32)], axis=0)

    def ada(w, b, l):
        return _matmul([c_all], w, layer=l, n_out=3 * d, tm=cond_rows, tn=_div_tile(3 * d, 1024, LANES),
                       tk=_div_tile(d, 2048, LANES), out_dtype=F32, bias=b.reshape(depth, 1, 3 * d),
                       silu_x=True, name="ada")

    def split_mod(mod):
        shift, scale, gate = mod[:, :d], mod[:, d:2 * d], mod[:, 2 * d:]
        per_p = [a[:bp].reshape(bp, 1, d) for a in (shift, scale, gate)]
        per_s = [jnp.repeat(a[bp:n_cond], ts, axis=0) for a in (shift, scale, gate)]
        return per_p, per_s

    tm_p = _div_tile(tp, 2048, 16)
    tm_s = ms
    tr_p = _div_tile(tp, 256, 16)
    tr_s = _div_tile(ms, 256, 16)

    def modulate_both(xp, xs, g, mod_p, mod_s, l):
        hp = _modulate(xp.reshape(bp, tp, d), g, mod_p[0], mod_p[1], layer=l, tr=tr_p)
        hs = _modulate(xs.reshape(1, ms, d), g, mod_s[0].reshape(1, ms, d), mod_s[1].reshape(1, ms, d),
                       layer=l, tr=tr_s)
        return hp, hs

    def residual_both(ap, as_, w, l, xp, xs, mod_p, mod_s, name):
        kdim = ap[0].shape[1]
        tk = _div_tile(kdim, 512, LANES)
        tn = _div_tile(d, 512, LANES)
        yp = _matmul(ap, w, layer=l, n_out=d, tm=tm_p, tn=tn, tk=tk, out_dtype=F32,
                     res=xp, gate=mod_p[2], gate_group_rows=tp, name=name + "_p")
        ys = _matmul(as_, w, layer=l, n_out=d, tm=tm_s, tn=tn, tk=tk, out_dtype=F32,
                     res=xs, gate=mod_s[2], name=name + "_s")
        return yp, ys

    gw = ev_ln_v_g.shape[-1]
    groups, gm_chunk = ev_w_s.shape[1], ev_w_s.shape[2]
    qr_rank = ev_g_qa.shape[-1]
    heads = ev_w_uk.shape[1]
    nope, rank = ev_w_uk.shape[2], ev_w_uk.shape[3]
    rope_d = ev_g_kr.shape[-1]
    ev_in = ev_w_in.shape[-1]
    cq_off, ckv_off, kr_off = 2 * gw, 2 * gw + qr_rank, 2 * gw + qr_rank + rank
    mla_scale = float(nope + rope_d) ** -0.5
    pos_p = jnp.tile(jnp.arange(tp), bp)
    pos_s = jnp.tile(past + jnp.arange(ts), bs)
    rope_p = _rope_tables(pos_p, rope_d)
    rope_s = _rope_tables(pos_s, rope_d)

    def mix_matrix(w_s, b_s, t):
        lc = min(t, gm_chunk)
        reps = gm_chunk // lc
        w = jnp.where(jnp.tril(jnp.ones((lc, lc), dtype=bool)), w_s[:, :lc, :lc], 0.0)
        w_blk = jnp.einsum("ab,gts->gatbs", jnp.eye(reps, dtype=F32), w).reshape(groups, gm_chunk, gm_chunk)
        b_blk = jnp.tile(b_s[:, :lc], (1, reps)).T
        return w_blk, b_blk

    ml_heads, ml_dk, ml_dv = state_mlstm_C.shape[2], state_mlstm_C.shape[3], state_mlstm_C.shape[4]
    lp = min(tp, ML_CHUNK)
    ncp = tp // lp

    xp = x_prompt.reshape(mp, d)
    xs = x_sample.reshape(ms, d)
    ckv_p, kr_p, ckv_s, kr_s, v_s = [], [], [], [], []
    cp_l, np_l, mp_l, cs_l, ns_l, ms_l = [], [], [], [], [], []
    for l in range(depth):
        j = l // 2
        mod_p, mod_s = split_mod(ada(ada_mix_w, ada_mix_b, l)[:n_cond])
        hp, hs = modulate_both(xp, xs, norm_mix_g.reshape(depth, 1, d), mod_p, mod_s, l)
        if l % 2 == 0:
            w_uq_h = ev_w_uq[j].reshape(qr_rank, heads, nope + rope_d).transpose(1, 0, 2)
            per_group = []
            for h_in, m_rows, t_len, ropes, is_sample in ((hp, mp, tp, rope_p, False), (hs, ms, ts, rope_s, True)):
                tm = _div_tile(m_rows, 2048, 16)
                z = _matmul([h_in], ev_w_in, layer=j, n_out=ev_in, tm=tm, tn=min(ev_in, 1024),
                            tk=_div_tile(d, 512, LANES), out_dtype=F32, name="ev_in")
                w_blk, b_blk = mix_matrix(ev_w_s[j], ev_b_s[j], t_len)
                a, v_ln = _gmlp(z, ev_ln_v_g.reshape(-1, 1, gw), ev_ln_v_b.reshape(-1, 1, gw), w_blk, b_blk,
                                layer=j, gw=gw, emit_v=is_sample)
                tmq = _div_tile(m_rows, 512, 16)
                qcat = _mla_q(z, ev_g_qa.reshape(-1, 1, qr_rank), w_uq_h, ev_g_qk.reshape(-1, 1, nope + rope_d),
                              ropes[0], ropes[1], ev_w_uk, layer=j, cq_off=cq_off, tm=tmq, scale=mla_scale)
                ckv_n, kr_n, kcat = _mla_kv(z, ev_g_kv.reshape(-1, 1, rank), ev_g_kr.reshape(-1, 1, rope_d),
                                            ropes[0], ropes[1], layer=j, ckv_off=ckv_off, kr_off=kr_off, tm=tmq)
                per_group.append((a, v_ln, qcat, ckv_n, kr_n, kcat))
            a_p, _, qcat_p, ckv_np, kr_np, kcat_p = per_group[0]
            a_s, v_ln_s, qcat_s, ckv_ns, kr_ns, kcat_s = per_group[1]
            o_p = _prompt_attention(qcat_p.reshape(heads, bp, tp, rank + rope_d),
                                    kcat_p.reshape(bp, tp, rank + rope_d), ev_w_uv,
                                    layer=j, tq=_div_tile(tp, 256, 16))
            q_s = qcat_s.reshape(heads, bs, ts, rank + rope_d).transpose(1, 0, 2, 3).reshape(bs, heads * ts, rank + rope_d)
            k_new = jnp.pad(kcat_s.reshape(bs, ts, rank + rope_d), ((0, 0), (0, LANES - ts), (0, 0)))
            o_lat_s = _sample_attention(q_s, k_new, cache_ckv, cache_krope, page_table, layer=j, t_new=ts)
            o_lat_s = o_lat_s.reshape(bs, heads, ts, rank).transpose(1, 0, 2, 3).reshape(heads, ms, rank)
            o_s = _value_up(o_lat_s, ev_w_uv, layer=j)
            xp, xs = residual_both([a_p, o_p], [a_s, o_s], ev_w_out, j, xp, xs, mod_p, mod_s, "ev_out")
            ckv_p.append(ckv_np.reshape(bp, tp, rank))
            kr_p.append(kr_np.reshape(bp, tp, rope_d))
            ckv_s.append(ckv_ns.reshape(bs, ts, rank))
            kr_s.append(kr_ns.reshape(bs, ts, rope_d))
            v_s.append(v_ln_s.reshape(bs, ts, gw))
        else:
            od_in = od_w_in.shape[-1]
            gate_off = od_in - 2 * ml_heads
            bg_c = od_b_gates.reshape(-1, 1, 2 * ml_heads)
            bg_r = od_b_gates.reshape(-1, 2 * ml_heads, 1)
            gmh = od_g_mh.reshape(-1, 1, ml_heads * ml_dv)
            results = []
            for h_in, m_rows, is_sample in ((hp, mp, False), (hs, ms, True)):
                tm = _div_tile(m_rows, 2048, 16)
                z = _matmul([h_in], od_w_in, layer=j, n_out=od_in, tm=tm, tn=min(od_in, 1024),
                            tk=_div_tile(d, 512, LANES), out_dtype=F32, name="od_in")
                if is_sample:
                    rows, nb, nc, valid = ML_SAMPLE_PAD, bs, 1, ts
                    z3 = jnp.pad(z.reshape(bs, ts, od_in), ((0, 0), (0, rows - ts), (0, 0)))
                    c0, n0, m0 = state_mlstm_C[j], state_mlstm_n[j], state_mlstm_m[j].reshape(bs, ml_heads, 1)
                else:
                    rows, nb, nc, valid = lp, bp, ncp, lp
                    z3 = z.reshape(bp * ncp, lp, od_in)
                    c0 = jnp.zeros((bp, ml_heads, ml_dk, ml_dv), F32)
                    n0 = jnp.zeros((bp, ml_heads, ml_dk), F32)
                    m0 = jnp.zeros((bp, ml_heads, 1), F32)
                g_rows = jnp.swapaxes(z3[:, :, gate_off:], 1, 2)
                h_out, c_n, n_n, m_n = _mlstm(z3, g_rows, bg_c, bg_r, gmh, c0, n0, m0,
                                              layer=j, nb=nb, nc=nc, valid=valid)
                results.append((h_out[:, :valid].reshape(m_rows, ml_heads * ml_dv), c_n, n_n,
                                m_n.reshape(nb, ml_heads)))
            (h_p, c_p, n_p, m_p), (h_s, c_s, n_s, m_s) = results
            xp, xs = residual_both([h_p], [h_s], od_w_out, j, xp, xs, mod_p, mod_s, "od_out")
            cp_l.append(c_p)
            np_l.append(n_p)
            mp_l.append(m_p)
            cs_l.append(c_s)
            ns_l.append(n_s)
            ms_l.append(m_s)

        mod_p, mod_s = split_mod(ada(ada_ffn_w, ada_ffn_b, l)[:n_cond])
        hp, hs = modulate_both(xp, xs, norm_ffn_g.reshape(depth, 1, d), mod_p, mod_s, l)
        act_p = _swiglu_in(hp, ffn_w_in, layer=l, tm=_div_tile(mp, 4096, 16), tk=_div_tile(d, 512, LANES), nsub=2)
        act_s = _swiglu_in(hs, ffn_w_in, layer=l, tm=ms, tk=_div_tile(d, 512, LANES), nsub=2)
        xp, xs = residual_both([act_p], [act_s], ffn_w_out, l, xp, xs, mod_p, mod_s, "ffn_out")

    return (xp.reshape(bp, tp, d), xs.reshape(bs, ts, d),
            jnp.stack(ckv_p), jnp.stack(kr_p), jnp.stack(ckv_s), jnp.stack(kr_s), jnp.stack(v_s),
            jnp.stack(cp_l), jnp.stack(np_l), jnp.stack(mp_l),
            jnp.stack(cs_l), jnp.stack(ns_l), jnp.stack(ms_l))
```

```python
import functools

import jax
import jax.numpy as jnp
from jax import lax
from jax.experimental import pallas as pl
from jax.experimental.pallas import tpu as pltpu

EPS = 1e-6
ROPE_THETA = 10000.0
GATE_CAP = 15.0
ML_CHUNK = 64
ML_SAMPLE_ROWS = 16
PAGES_PER_STEP = 16
LANES = 128
SUBLANES = 8
V7X_VMEM_LIMIT = 56 * 1024 * 1024
MM_TM_PREF = 2176
SWIGLU_TM_PREF = 4352
ATTN_TQ_PREF = 256
ATTN_HEADS_PER_STEP = 8

F32 = jnp.float32
BF16 = jnp.bfloat16


def _cparams(*sem):
    return pltpu.CompilerParams(dimension_semantics=sem, vmem_limit_bytes=V7X_VMEM_LIMIT)


def _div_tile(dim, pref, align):
    if dim <= pref:
        return dim
    t = (pref // align) * align
    while t >= align:
        if dim % t == 0:
            return t
        t -= align
    return dim


def _k_tile(kdim, pref):
    exact = _div_tile(kdim, pref, 256)
    if kdim <= pref or 2 * exact > pref:
        return exact, 0
    for tk in range((pref // 256) * 256, exact, -256):
        rem = kdim % tk
        if rem and (kdim - rem) % rem == 0 and rem % LANES == 0:
            return tk, rem
    return exact, 0


def _rms(x, g):
    return x * lax.rsqrt(jnp.mean(x * x, axis=-1, keepdims=True) + EPS) * g


def _gelu(x):
    return x * (0.5 * (1.0 + jnp.tanh(0.7978845608028654 * (x + 0.044715 * (x * x * x)))))


def _sigmoid(x):
    return 1.0 / (1.0 + jnp.exp(-x))


def _dot(a, b):
    return jnp.dot(a, b, preferred_element_type=F32)


def _dot_nt(a, b):
    return lax.dot_general(a, b, (((1,), (1,)), ((), ())), preferred_element_type=F32)


def _dot_tn(a, b):
    return lax.dot_general(a, b, (((0,), (0,)), ((), ())), preferred_element_type=F32)


def _row_segments(block_start, tm, n_prompt_rows, tp):
    segs, r, end = [], block_start, block_start + tm
    while r < end:
        if r < n_prompt_rows:
            seq = r // tp
            hi = min(end, (seq + 1) * tp)
            segs.append((r - block_start, hi - block_start, "p", seq))
        else:
            hi = end
            segs.append((r - block_start, hi - block_start, "s", r - n_prompt_rows))
        r = hi
    return segs


def _mm_kernel(*refs, nk, nx, nka, w_t, has_rem, silu_x, has_bias, segments):
    it = iter(refs)
    x_refs = [next(it) for _ in range(nx)]
    w_ref = next(it)
    xr_ref = next(it) if has_rem else None
    wr_ref = next(it) if has_rem else None
    b_ref = next(it) if has_bias else None
    res_ref = gp_ref = gs_ref = None
    if segments is not None:
        res_ref, gp_ref, gs_ref = next(it), next(it), next(it)
    o_ref = next(it)
    acc_ref = next(it)
    k = pl.program_id(2)

    @pl.when(k == 0)
    def _init():
        if has_rem:
            acc_ref[...] = _dot(xr_ref[...].astype(BF16), wr_ref[...].astype(BF16))
        else:
            acc_ref[...] = jnp.zeros_like(acc_ref)

    def accum(x_ref):
        x = x_ref[...]
        if silu_x:
            x = x * _sigmoid(x)
        w = w_ref[...].astype(BF16)
        acc_ref[...] += _dot_nt(x.astype(BF16), w) if w_t else _dot(x.astype(BF16), w)

    if nx == 1:
        accum(x_refs[0])
    else:
        pl.when(k < nka)(lambda: accum(x_refs[0]))
        pl.when(k >= nka)(lambda: accum(x_refs[1]))

    @pl.when(k == nk - 1)
    def _finish():
        if segments is None:
            y = acc_ref[...]
            if has_bias:
                y = y + b_ref[...]
            o_ref[...] = y.astype(o_ref.dtype)
        else:
            i = pl.program_id(0)
            for bi, segs in enumerate(segments):
                @pl.when(i == bi)
                def _gated(segs=segs):
                    for lo, hi, kind, idx in segs:
                        gate = gp_ref[idx:idx + 1, :] if kind == "p" else gs_ref[idx:idx + hi - lo, :]
                        o_ref[lo:hi, :] = res_ref[lo:hi, :] + gate * acc_ref[lo:hi, :]


def _matmul(xs, w, *, layer, n_out, tm, tn, tk, out_dtype, w_t=False, k_rem=0, bias=None,
            silu_x=False, resid=None, name="mm"):
    m, k_each = xs[0].shape
    nx = len(xs)
    assert m % tm == 0 and (k_each - k_rem) % tk == 0 and (nx == 1 or k_rem == 0)
    nka = (k_each - k_rem) // tk
    nk = nka * nx
    grid = (m // tm, pl.cdiv(n_out, tn), nk)

    in_specs = [pl.BlockSpec((tm, tk), lambda i, j, k: (i, jnp.minimum(k, nka - 1)))]
    if nx == 2:
        in_specs.append(pl.BlockSpec((tm, tk), lambda i, j, k: (i, jnp.maximum(k - nka, 0))))
    if w_t:
        in_specs.append(pl.BlockSpec((None, tn, tk), lambda i, j, k: (layer, j, k)))
    else:
        in_specs.append(pl.BlockSpec((None, tk, tn), lambda i, j, k: (layer, k, j)))
    args = list(xs) + [w]
    if k_rem:
        assert not w_t
        rem_blk = (k_each - k_rem) // k_rem
        in_specs.append(pl.BlockSpec((tm, k_rem), lambda i, j, k: (i, rem_blk)))
        in_specs.append(pl.BlockSpec((None, k_rem, tn), lambda i, j, k: (layer, rem_blk, j)))
        args += [xs[0], w]
    if bias is not None:
        in_specs.append(pl.BlockSpec((None, 1, tn), lambda i, j, k: (layer, 0, j)))
        args.append(bias)
    segments = None
    if resid is not None:
        n_prompt_rows, tp = resid["bp"] * resid["tp"], resid["tp"]
        gate_blk = resid["gate_col"] // tn
        assert resid["gate_col"] % tn == 0 and resid["bp"] <= SUBLANES
        segments = [_row_segments(b * tm, tm, n_prompt_rows, tp) for b in range(m // tm)]
        ms = resid["gate_s"].shape[0]
        in_specs += [
            pl.BlockSpec((tm, tn), lambda i, j, k: (i, j)),
            pl.BlockSpec((SUBLANES, tn), lambda i, j, k: (0, gate_blk + j)),
            pl.BlockSpec((ms, tn), lambda i, j, k: (0, gate_blk + j)),
        ]
        args += [resid["res"], resid["mod"], resid["gate_s"]]
    kern = functools.partial(_mm_kernel, nk=nk, nx=nx, nka=nka, w_t=w_t, has_rem=bool(k_rem), silu_x=silu_x,
                             has_bias=bias is not None, segments=segments)
    return pl.pallas_call(
        kern,
        grid=grid,
        in_specs=in_specs,
        out_specs=pl.BlockSpec((tm, tn), lambda i, j, k: (i, j)),
        out_shape=jax.ShapeDtypeStruct((m, n_out), out_dtype),
        scratch_shapes=[pltpu.VMEM((tm, tn), F32)],
        compiler_params=_cparams("parallel", "parallel", "arbitrary"),
        name=name,
    )(*args)


SWIGLU_SUB = 256


def _swiglu_kernel(x_ref, *refs, nk, nsub):
    wg = refs[:nsub]
    wu = refs[nsub:2 * nsub]
    o_ref = refs[2 * nsub]
    accg, accu = refs[2 * nsub + 1:]
    k = pl.program_id(2)

    @pl.when(k == 0)
    def _zero():
        accg[...] = jnp.zeros_like(accg)
        accu[...] = jnp.zeros_like(accu)

    x = x_ref[...]
    for s in range(nsub):
        cols = slice(s * SWIGLU_SUB, (s + 1) * SWIGLU_SUB)
        accg[:, cols] += _dot(x, wg[s][...].astype(BF16))
        accu[:, cols] += _dot(x, wu[s][...].astype(BF16))

    @pl.when(k == nk - 1)
    def _finish():
        g = accg[...]
        o_ref[...] = (g * _sigmoid(g) * accu[...]).astype(o_ref.dtype)


def _swiglu_in(x, w, *, layer, tm, tk, nsub):
    m, kdim = x.shape
    f = w.shape[2] // 2
    sub = SWIGLU_SUB
    assert f % sub == 0 and m % tm == 0 and kdim % tk == 0
    nfb = f // sub
    last = 2 * nfb - 1
    nk = kdim // tk
    tn = nsub * sub
    grid = (m // tm, pl.cdiv(f, tn), nk)
    in_specs = [pl.BlockSpec((tm, tk), lambda i, j, k: (i, k))]
    for s in range(nsub):
        in_specs.append(pl.BlockSpec((None, tk, sub), lambda i, j, k, s=s: (layer, k, j * nsub + s)))
    for s in range(nsub):
        in_specs.append(pl.BlockSpec(
            (None, tk, sub), lambda i, j, k, s=s: (layer, k, jnp.minimum(nfb + j * nsub + s, last))))
    return pl.pallas_call(
        functools.partial(_swiglu_kernel, nk=nk, nsub=nsub),
        grid=grid,
        in_specs=in_specs,
        out_specs=pl.BlockSpec((tm, tn), lambda i, j, k: (i, j)),
        out_shape=jax.ShapeDtypeStruct((m, f), BF16),
        scratch_shapes=[pltpu.VMEM((tm, tn), F32), pltpu.VMEM((tm, tn), F32)],
        compiler_params=_cparams("parallel", "parallel", "arbitrary"),
        name="swiglu_in",
    )(x, *([w] * (2 * nsub)))


def _modulate_kernel(x_ref, g_ref, shp_ref, scp_ref, shs_ref, scs_ref, o_ref, *, tr, tp, bp, npb):
    i = pl.program_id(0)
    seq = jnp.minimum(i * tr // tp, bp - 1)
    is_sample = i >= npb
    shift = jnp.where(is_sample, shs_ref[...], shp_ref[pl.ds(seq, 1), :])
    scale = jnp.where(is_sample, scs_ref[...], scp_ref[pl.ds(seq, 1), :])
    y = _rms(x_ref[...], g_ref[...])
    o_ref[...] = (y * (1.0 + scale) + shift).astype(o_ref.dtype)


def _modulate(x, g, mod, mod_s, *, layer, tr, bp, tp):
    m, d = x.shape
    npb = bp * tp // tr
    srow = lambda i: jnp.maximum(i - npb, 0)
    return pl.pallas_call(
        functools.partial(_modulate_kernel, tr=tr, tp=tp, bp=bp, npb=npb),
        grid=(m // tr,),
        in_specs=[
            pl.BlockSpec((tr, d), lambda i: (i, 0)),
            pl.BlockSpec((None, 1, d), lambda i: (layer, 0, 0)),
            pl.BlockSpec((SUBLANES, d), lambda i: (0, 0)),
            pl.BlockSpec((SUBLANES, d), lambda i: (0, 1)),
            pl.BlockSpec((tr, d), lambda i: (srow(i), 0)),
            pl.BlockSpec((tr, d), lambda i: (srow(i), 1)),
        ],
        out_specs=pl.BlockSpec((tr, d), lambda i: (i, 0)),
        out_shape=jax.ShapeDtypeStruct((m, d), BF16),
        compiler_params=_cparams("parallel"),
        name="modulate",
    )(x, g, mod, mod, mod_s, mod_s)


def _gmlp_kernel(u_ref, v_ref, lng_ref, lnb_ref, w_ref, b_ref, a_ref, vout_ref, *, groups, gdim):
    for g in range(groups):
        cols = slice(g * gdim, (g + 1) * gdim)
        vg = _gelu(v_ref[:, cols])
        mu = jnp.mean(vg, axis=-1, keepdims=True)
        xc = vg - mu
        var = jnp.mean(xc * xc, axis=-1, keepdims=True)
        y = xc * lax.rsqrt(var + EPS) * lng_ref[:, cols] + lnb_ref[:, cols]
        vout_ref[:, cols] = y
        mixed = _dot(w_ref[g].astype(BF16), y.astype(BF16)) + b_ref[:, g:g + 1]
        a_ref[:, cols] = (_gelu(u_ref[:, cols]) * mixed).astype(a_ref.dtype)


def _gmlp(z, ln_g, ln_b, w_mix, b_mix, *, layer, gw, n_prompt_blocks):
    m = z.shape[0]
    groups, rows = w_mix.shape[1], w_mix.shape[2]
    assert m % rows == 0
    which = lambda i: jnp.where(i < n_prompt_blocks, 0, 1)
    return pl.pallas_call(
        functools.partial(_gmlp_kernel, groups=groups, gdim=gw // groups),
        grid=(m // rows,),
        in_specs=[
            pl.BlockSpec((rows, gw), lambda i: (i, 0)),
            pl.BlockSpec((rows, gw), lambda i: (i, 1)),
            pl.BlockSpec((None, 1, gw), lambda i: (layer, 0, 0)),
            pl.BlockSpec((None, 1, gw), lambda i: (layer, 0, 0)),
            pl.BlockSpec((None, groups, rows, rows), lambda i: (which(i), 0, 0, 0)),
            pl.BlockSpec((None, rows, groups), lambda i: (which(i), 0, 0)),
        ],
        out_specs=[pl.BlockSpec((rows, gw), lambda i: (i, 0)), pl.BlockSpec((rows, gw), lambda i: (i, 0))],
        out_shape=[jax.ShapeDtypeStruct((m, gw), BF16), jax.ShapeDtypeStruct((m, gw), F32)],
        compiler_params=_cparams("parallel"),
        name="gmlp",
    )(z, z, ln_g, ln_b, w_mix, b_mix)


def _rope_rows(x, cosf, sinf):
    half = x.shape[-1] // 2
    swapped = jnp.concatenate([x[:, half:], x[:, :half]], axis=-1)
    return x * cosf + swapped * sinf


def _q_kernel(cq_ref, gqa_ref, wuq_ref, gqk_ref, cos_ref, sin_ref, wuk_ref, o_ref, cqn_sc,
              *, hb, nope, rank, scale):
    @pl.when(pl.program_id(1) == 0)
    def _norm():
        cqn_sc[...] = _rms(cq_ref[...], gqa_ref[...]).astype(BF16)

    for h in range(hb):
        q = _rms(_dot(cqn_sc[...], wuq_ref[h].astype(BF16)), gqk_ref[...])
        q_rope = _rope_rows(q[:, nope:], cos_ref[...], sin_ref[...])
        q_lat = _dot(q[:, :nope].astype(BF16), wuk_ref[h].astype(BF16))
        o_ref[h, :, :rank] = (q_lat * scale).astype(o_ref.dtype)
        o_ref[h, :, rank:] = (q_rope * scale).astype(o_ref.dtype)


def _mla_q(z, g_qa, w_uq_h, g_qk, cosf, sinf, w_uk, *, layer, cq_off, tm, hb, scale):
    m = z.shape[0]
    qr = g_qa.shape[-1]
    heads, _, qd = w_uq_h.shape
    nope, rank = w_uk.shape[2], w_uk.shape[3]
    rope_d = qd - nope
    assert cq_off % qr == 0 and heads % hb == 0
    cq_blk = cq_off // qr
    return pl.pallas_call(
        functools.partial(_q_kernel, hb=hb, nope=nope, rank=rank, scale=scale),
        grid=(m // tm, heads // hb),
        in_specs=[
            pl.BlockSpec((tm, qr), lambda i, h: (i, cq_blk)),
            pl.BlockSpec((None, 1, qr), lambda i, h: (layer, 0, 0)),
            pl.BlockSpec((hb, qr, qd), lambda i, h: (h, 0, 0)),
            pl.BlockSpec((None, 1, qd), lambda i, h: (layer, 0, 0)),
            pl.BlockSpec((tm, rope_d), lambda i, h: (i, 0)),
            pl.BlockSpec((tm, rope_d), lambda i, h: (i, 0)),
            pl.BlockSpec((None, hb, nope, rank), lambda i, h: (layer, h, 0, 0)),
        ],
        out_specs=pl.BlockSpec((hb, tm, rank + rope_d), lambda i, h: (h, i, 0)),
        out_shape=jax.ShapeDtypeStruct((heads, m, rank + rope_d), BF16),
        scratch_shapes=[pltpu.VMEM((tm, qr), BF16)],
        compiler_params=_cparams("parallel", "arbitrary"),
        name="mla_q",
    )(z, g_qa, w_uq_h, g_qk, cosf, sinf, w_uk)


def _kv_kernel(ckv_ref, kr_ref, gkv_ref, gkr_ref, cos_ref, sin_ref, ckv_o, kr_o, kcat_o,
               *, rank, rope_d):
    c = _rms(ckv_ref[...], gkv_ref[...])
    r = _rope_rows(_rms(kr_ref[:, :rope_d], gkr_ref[...]), cos_ref[...], sin_ref[...])
    ckv_o[...] = c
    kr_o[...] = r
    kcat_o[:, :rank] = c.astype(kcat_o.dtype)
    kcat_o[:, rank:] = r.astype(kcat_o.dtype)


def _mla_kv(z, g_kv, g_kr, cosf, sinf, *, layer, ckv_off, kr_off, tm):
    m = z.shape[0]
    rank = g_kv.shape[-1]
    rope_d = g_kr.shape[-1]
    assert ckv_off % rank == 0 and kr_off % LANES == 0
    return pl.pallas_call(
        functools.partial(_kv_kernel, rank=rank, rope_d=rope_d),
        grid=(m // tm,),
        in_specs=[
            pl.BlockSpec((tm, rank), lambda i: (i, ckv_off // rank)),
            pl.BlockSpec((tm, LANES), lambda i: (i, kr_off // LANES)),
            pl.BlockSpec((None, 1, rank), lambda i: (layer, 0, 0)),
            pl.BlockSpec((None, 1, rope_d), lambda i: (layer, 0, 0)),
            pl.BlockSpec((tm, rope_d), lambda i: (i, 0)),
            pl.BlockSpec((tm, rope_d), lambda i: (i, 0)),
        ],
        out_specs=[
            pl.BlockSpec((tm, rank), lambda i: (i, 0)),
            pl.BlockSpec((tm, rope_d), lambda i: (i, 0)),
            pl.BlockSpec((tm, rank + rope_d), lambda i: (i, 0)),
        ],
        out_shape=[
            jax.ShapeDtypeStruct((m, rank), F32),
            jax.ShapeDtypeStruct((m, rope_d), F32),
            jax.ShapeDtypeStruct((m, rank + rope_d), BF16),
        ],
        compiler_params=_cparams("parallel"),
        name="mla_kv",
    )(z, z, g_kv, g_kr, cosf, sinf)


def _pattn_kernel(q_ref, k_ref, wuv_ref, o_ref, m_sc, l_sc, acc_sc, *, heads, tq, rank, vd):
    qi = pl.program_id(2)
    rows = heads * tq
    qd = q_ref.shape[-1]

    m_sc[...] = jnp.full_like(m_sc, -jnp.inf)
    l_sc[...] = jnp.zeros_like(l_sc)
    acc_sc[...] = jnp.zeros_like(acc_sc)

    def step(kb, masked):
        q = q_ref[...].reshape(rows, qd)
        kblk = k_ref[pl.ds(pl.multiple_of(kb * tq, tq), tq), :]
        s = _dot_nt(q, kblk)
        if masked:
            row = lax.rem(lax.broadcasted_iota(jnp.int32, s.shape, 0), tq)
            col = lax.broadcasted_iota(jnp.int32, s.shape, 1)
            s = jnp.where(col <= row, s, -jnp.inf)
        m_prev = m_sc[...]
        m_new = jnp.maximum(m_prev, jnp.max(s, axis=-1, keepdims=True))
        alpha = jnp.exp(m_prev - m_new)
        p = jnp.exp(s - m_new)
        l_sc[...] = alpha * l_sc[...] + jnp.sum(p, axis=-1, keepdims=True)
        acc_sc[...] = alpha * acc_sc[...] + _dot(p.astype(BF16), kblk[:, :rank])
        m_sc[...] = m_new

    def body(kb, carry):
        step(kb, False)
        return carry

    lax.fori_loop(0, qi, body, 0)
    step(qi, True)
    for h in range(heads):
        rs = slice(h * tq, (h + 1) * tq)
        o_lat = (acc_sc[rs, :] / l_sc[rs, :]).astype(BF16)
        o_ref[:, h * vd:(h + 1) * vd] = _dot(o_lat, wuv_ref[h].astype(BF16)).astype(o_ref.dtype)


def _prompt_attention(qcat, kcat, w_uv, *, layer, bp, tp, tq):
    heads, m, qd = qcat.shape
    rank, vd = w_uv.shape[2], w_uv.shape[3]
    nq = tp // tq
    hg = _div_tile(heads, ATTN_HEADS_PER_STEP, 1)
    return pl.pallas_call(
        functools.partial(_pattn_kernel, heads=hg, tq=tq, rank=rank, vd=vd),
        grid=(bp, heads // hg, nq),
        in_specs=[
            pl.BlockSpec((hg, tq, qd), lambda b, g, qi: (g, b * nq + qi, 0)),
            pl.BlockSpec((tp, qd), lambda b, g, qi: (b, 0)),
            pl.BlockSpec((None, hg, rank, vd), lambda b, g, qi: (layer, g, 0, 0)),
        ],
        out_specs=pl.BlockSpec((tq, hg * vd), lambda b, g, qi: (b * nq + qi, g)),
        out_shape=jax.ShapeDtypeStruct((m, heads * vd), BF16),
        scratch_shapes=[pltpu.VMEM((hg * tq, 1), F32), pltpu.VMEM((hg * tq, 1), F32),
                        pltpu.VMEM((hg * tq, rank), F32)],
        compiler_params=_cparams("parallel", "parallel", "arbitrary"),
        name="prompt_attn",
    )(qcat, kcat, w_uv)


def _sattn_kernel(pt_ref, q_ref, knew_ref, *refs, pb, page, rank, nsteps, t_new):
    del pt_ref
    ckv_refs = refs[:pb]
    krt_refs = refs[pb:2 * pb]
    o_ref = refs[2 * pb]
    kbuf, krt_buf, m_sc, l_sc, acc_sc = refs[2 * pb + 1:]
    step = pl.program_id(1)

    @pl.when(step == 0)
    def _init():
        m_sc[...] = jnp.full_like(m_sc, -jnp.inf)
        l_sc[...] = jnp.zeros_like(l_sc)
        acc_sc[...] = jnp.zeros_like(acc_sc)

    def online_update(s, values):
        m_prev = m_sc[...]
        m_new = jnp.maximum(m_prev, jnp.max(s, axis=-1, keepdims=True))
        alpha = jnp.exp(m_prev - m_new)
        p = jnp.exp(s - m_new)
        l_sc[...] = alpha * l_sc[...] + jnp.sum(p, axis=-1, keepdims=True)
        acc_sc[...] = alpha * acc_sc[...] + _dot(p.astype(BF16), values)
        m_sc[...] = m_new

    for i in range(pb):
        kbuf[i * page:(i + 1) * page, :] = ckv_refs[i][...].astype(BF16)
        krt_buf[:, i * page:(i + 1) * page] = krt_refs[i][...].astype(BF16)
    q = q_ref[...]
    s = _dot_nt(q[:, :rank], kbuf[...]) + _dot(q[:, rank:], krt_buf[...])
    online_update(s, kbuf[...])

    @pl.when(step == nsteps - 1)
    def _finish():
        k_new = knew_ref[...]
        s_new = _dot_nt(q, k_new)
        row = lax.rem(lax.broadcasted_iota(jnp.int32, s_new.shape, 0), t_new)
        col = lax.broadcasted_iota(jnp.int32, s_new.shape, 1)
        online_update(jnp.where(col <= row, s_new, -jnp.inf), k_new[:, :rank])
        o_ref[...] = (acc_sc[...] / l_sc[...]).astype(o_ref.dtype)


def _sample_attention(q_s, k_new, cache_ckv, cache_krope_t, page_table, *, layer, t_new):
    b, rows, qd = q_s.shape
    page, rank = cache_ckv.shape[2], cache_ckv.shape[3]
    rope_d = cache_krope_t.shape[2]
    n_pages = page_table.shape[1]
    pb = _div_tile(n_pages, PAGES_PER_STEP, 1)
    nsteps = n_pages // pb
    in_specs = [
        pl.BlockSpec((None, rows, qd), lambda bi, s, pt: (bi, 0, 0)),
        pl.BlockSpec((None, k_new.shape[1], qd), lambda bi, s, pt: (bi, 0, 0)),
    ]
    for i in range(pb):
        in_specs.append(pl.BlockSpec(
            (None, None, page, rank),
            lambda bi, s, pt, i=i: (layer, pt[bi * n_pages + s * pb + i], 0, 0)))
    for i in range(pb):
        in_specs.append(pl.BlockSpec(
            (None, None, rope_d, page),
            lambda bi, s, pt, i=i: (layer, pt[bi * n_pages + s * pb + i], 0, 0)))
    grid_spec = pltpu.PrefetchScalarGridSpec(
        num_scalar_prefetch=1,
        grid=(b, nsteps),
        in_specs=in_specs,
        out_specs=pl.BlockSpec((None, rows, rank), lambda bi, s, pt: (bi, 0, 0)),
        scratch_shapes=[pltpu.VMEM((pb * page, rank), BF16), pltpu.VMEM((rope_d, pb * page), BF16),
                        pltpu.VMEM((rows, 1), F32), pltpu.VMEM((rows, 1), F32),
                        pltpu.VMEM((rows, rank), F32)],
    )
    return pl.pallas_call(
        functools.partial(_sattn_kernel, pb=pb, page=page, rank=rank, nsteps=nsteps, t_new=t_new),
        grid_spec=grid_spec,
        out_shape=jax.ShapeDtypeStruct((b, rows, rank), BF16),
        compiler_params=_cparams("parallel", "arbitrary"),
        name="sample_attn",
    )(page_table.reshape(-1), q_s, k_new, *([cache_ckv] * pb), *([cache_krope_t] * pb))


def _uv_kernel(x_ref, w_ref, prev_ref, o_ref):
    del prev_ref
    o_ref[...] = _dot(x_ref[...], w_ref[...].astype(BF16)).astype(o_ref.dtype)


def _value_up(o_lat, w_uv, o_prev, *, layer, row0):
    heads, ms, rank = o_lat.shape
    vd = w_uv.shape[3]
    assert row0 % ms == 0
    return pl.pallas_call(
        _uv_kernel,
        grid=(heads,),
        in_specs=[
            pl.BlockSpec((None, ms, rank), lambda h: (h, 0, 0)),
            pl.BlockSpec((None, None, rank, vd), lambda h: (layer, h, 0, 0)),
            pl.BlockSpec(memory_space=pl.ANY),
        ],
        out_specs=pl.BlockSpec((ms, vd), lambda h: (row0 // ms, h)),
        out_shape=jax.ShapeDtypeStruct(o_prev.shape, o_prev.dtype),
        input_output_aliases={2: 0},
        compiler_params=_cparams("parallel"),
        name="value_up",
    )(o_lat, w_uv, o_prev)


def _mlstm_kernel(*refs, heads, dk, dv, rows, sub_rows, nsub, aliased):
    (q_ref, k_ref, v_ref, o_ref, gc_ref, gr_ref, bgc_ref, bgr_ref, gmh_ref, c0_ref, n0_ref, m0_ref) = refs[:12]
    h_ref, c_ref, n_ref, m_ref = refs[-4:]
    sub = pl.program_id(2)

    @pl.when(pl.program_id(1) == 0)
    def _load_state():
        c_ref[...] = c0_ref[...]
        n_ref[...] = n0_ref[...]
        m_ref[...] = m0_ref[...]

    g_col = GATE_CAP * jnp.tanh((gc_ref[:, :2 * heads] + bgc_ref[...]) / GATE_CAP)
    g_row = GATE_CAP * jnp.tanh((gr_ref[...] + bgr_ref[...]) / GATE_CAP)

    def log_sigmoid(x):
        return jnp.minimum(x, 0.0) - jnp.log(1.0 + jnp.exp(-jnp.abs(x)))

    li = lax.broadcasted_iota(jnp.int32, (rows, rows), 0)
    si = lax.broadcasted_iota(jnp.int32, (rows, rows), 1)
    causal = si <= li
    ri = lax.broadcasted_iota(jnp.int32, (rows, 1), 0)
    ci = lax.broadcasted_iota(jnp.int32, (1, rows), 1)
    lo, hi = sub * sub_rows, (sub + 1) * sub_rows
    row_ok = (ri >= lo) & (ri < hi)
    col_ok = (ci >= lo) & (ci < hi)
    k_scale = dk ** -0.5

    for h in range(heads):
        ig_c = jnp.where(row_ok, g_col[:, h:h + 1], -jnp.inf)
        lf_c = jnp.where(row_ok, log_sigmoid(g_col[:, heads + h:heads + h + 1]), 0.0)
        ig_r = jnp.where(col_ok, g_row[h:h + 1, :], -jnp.inf)
        lf_r = jnp.where(col_ok, log_sigmoid(g_row[heads + h:heads + h + 1, :]), 0.0)
        b_c = jnp.sum(jnp.where(causal, lf_r, 0.0), axis=1, keepdims=True)
        b_r = jnp.sum(jnp.where(li <= si, lf_c, 0.0), axis=0, keepdims=True)
        b_last = jnp.sum(lf_r, axis=1, keepdims=True)
        m0 = m_ref[h:h + 1, :]
        n0 = n_ref[h:h + 1, :]
        c0 = c_ref[h]

        qh = q_ref[:, h * dk:(h + 1) * dk]
        kh = k_ref[:, h * dk:(h + 1) * dk] * k_scale
        vh = v_ref[:, h * dv:(h + 1) * dv]
        qb = qh.astype(BF16)
        kb = kh.astype(BF16)
        vb = vh.astype(BF16)

        dmat = jnp.where(causal, b_c - b_r + ig_r, -jnp.inf)
        inter = b_c + m0
        m_pos = jnp.maximum(inter, jnp.max(dmat, axis=1, keepdims=True))
        w = jnp.exp(dmat - m_pos) * _dot_nt(qb, kb)
        a = jnp.exp(inter - m_pos)
        num = a * _dot(qb, c0.astype(BF16)) + _dot(w.astype(BF16), vb)
        nq = a * jnp.sum(qh * n0, axis=1, keepdims=True) + jnp.sum(w, axis=1, keepdims=True)
        den = jnp.maximum(jnp.abs(nq), jnp.exp(-m_pos))
        hs = num / den

        g_r = b_last - b_r + ig_r
        m_new = jnp.maximum(b_last + m0, jnp.max(g_r, axis=1, keepdims=True))
        decay = jnp.exp(b_last + m0 - m_new)
        wk_c = jnp.exp(b_last - b_c + ig_c - m_new)
        kw = wk_c * kh
        c_ref[h] = decay * c0 + _dot_tn(kw.astype(BF16), vb)
        n_ref[h:h + 1, :] = decay * n0 + jnp.sum(kw, axis=0, keepdims=True)
        m_ref[h:h + 1, :] = m_new

        hn = _rms(hs, gmh_ref[:, h * dv:(h + 1) * dv])
        out = (_sigmoid(o_ref[:, h * dv:(h + 1) * dv]) * hn).astype(h_ref.dtype)
        cols = slice(h * dv, (h + 1) * dv)
        if nsub == 1:
            h_ref[:, cols] = out
        else:
            @pl.when(sub == 0)
            def _first(out=out, cols=cols):
                h_ref[:, cols] = jnp.where(row_ok, out, jnp.zeros_like(out))

            @pl.when(sub > 0)
            def _rest(out=out, cols=cols):
                h_ref[:, cols] = jnp.where(row_ok, out, h_ref[:, cols])


def _mlstm(z, g_rows, b_gates_c, b_gates_r, g_mh, c0, n0, m0, h_prev, *, layer, nb, nc, nsub,
           rows, sub_rows, row_blk0):
    m = z.shape[0]
    _, heads, dk, dv = c0.shape
    qw, vw = heads * dk, heads * dv
    assert (2 * qw) % vw == 0 and (2 * qw + 2 * vw) % LANES == 0
    g_blk = (2 * qw + 2 * vw) // LANES
    blk = lambda b, c, s: row_blk0 + b * nc + c
    seq = lambda b, c, s: b * nsub + s
    in_specs = [
        pl.BlockSpec((rows, qw), lambda b, c, s: (blk(b, c, s), 0)),
        pl.BlockSpec((rows, qw), lambda b, c, s: (blk(b, c, s), 1)),
        pl.BlockSpec((rows, vw), lambda b, c, s: (blk(b, c, s), 2 * qw // vw)),
        pl.BlockSpec((rows, vw), lambda b, c, s: (blk(b, c, s), 2 * qw // vw + 1)),
        pl.BlockSpec((rows, LANES), lambda b, c, s: (blk(b, c, s), g_blk)),
        pl.BlockSpec((None, 2 * heads, rows), lambda b, c, s: (b * nc + c, 0, 0)),
        pl.BlockSpec((None, 1, 2 * heads), lambda b, c, s: (layer, 0, 0)),
        pl.BlockSpec((None, 2 * heads, 1), lambda b, c, s: (layer, 0, 0)),
        pl.BlockSpec((None, 1, vw), lambda b, c, s: (layer, 0, 0)),
        pl.BlockSpec((None, heads, dk, dv), lambda b, c, s: (seq(b, c, s), 0, 0, 0)),
        pl.BlockSpec((None, heads, dk), lambda b, c, s: (seq(b, c, s), 0, 0)),
        pl.BlockSpec((None, heads, 1), lambda b, c, s: (seq(b, c, s), 0, 0)),
    ]
    args = [z, z, z, z, z, g_rows, b_gates_c, b_gates_r, g_mh, c0, n0, m0]
    aliases = {}
    if h_prev is not None:
        in_specs.append(pl.BlockSpec(memory_space=pl.ANY))
        args.append(h_prev)
        aliases = {12: 0}
    return pl.pallas_call(
        functools.partial(_mlstm_kernel, heads=heads, dk=dk, dv=dv, rows=rows, sub_rows=sub_rows,
                          nsub=nsub, aliased=h_prev is not None),
        grid=(nb, nc, nsub),
        in_specs=in_specs,
        out_specs=[
            pl.BlockSpec((rows, vw), lambda b, c, s: (blk(b, c, s), 0)),
            pl.BlockSpec((None, heads, dk, dv), lambda b, c, s: (seq(b, c, s), 0, 0, 0)),
            pl.BlockSpec((None, heads, dk), lambda b, c, s: (seq(b, c, s), 0, 0)),
            pl.BlockSpec((None, heads, 1), lambda b, c, s: (seq(b, c, s), 0, 0)),
        ],
        out_shape=[
            jax.ShapeDtypeStruct((m, vw), BF16),
            jax.ShapeDtypeStruct((nb * nsub, heads, dk, dv), F32),
            jax.ShapeDtypeStruct((nb * nsub, heads, dk), F32),
            jax.ShapeDtypeStruct((nb * nsub, heads, 1), F32),
        ],
        input_output_aliases=aliases,
        compiler_params=_cparams("parallel", "arbitrary", "arbitrary"),
        name="mlstm",
    )(*args)


def _rope_tables(pos, rope_d):
    half = rope_d // 2
    inv = ROPE_THETA ** (-jnp.arange(half, dtype=F32) * (2.0 / rope_d))
    ang = pos.astype(F32)[:, None] * inv[None, :]
    cos, sin = jnp.cos(ang), jnp.sin(ang)
    return jnp.concatenate([cos, cos], axis=-1), jnp.concatenate([-sin, sin], axis=-1)


def kernel(x_prompt, x_sample, c_prompt, c_sample, cache_ckv, cache_krope, page_table,
           state_mlstm_C, state_mlstm_n, state_mlstm_m,
           norm_mix_g, ada_mix_w, ada_mix_b, norm_ffn_g, ada_ffn_w, ada_ffn_b, ffn_w_in, ffn_w_out,
           ev_w_in, ev_ln_v_g, ev_ln_v_b, ev_w_s, ev_b_s, ev_g_qa, ev_w_uq, ev_g_qk, ev_g_kv, ev_g_kr,
           ev_w_uk, ev_w_uv, ev_w_out, od_w_in, od_b_gates, od_g_mh, od_w_out):
    bp, tp, d = x_prompt.shape
    bs, ts, _ = x_sample.shape
    depth = norm_mix_g.shape[0]
    mp, ms = bp * tp, bs * ts
    m = mp + ms
    past = page_table.shape[1] * cache_ckv.shape[2]

    n_cond = bp + bs
    cond_rows = -(-n_cond // 16) * 16
    c_all = jnp.concatenate([c_prompt, c_sample, jnp.zeros((cond_rows - n_cond, d), F32)], axis=0)

    def ada(w, b, l):
        mod = _matmul([c_all], w, layer=l, n_out=3 * d, tm=cond_rows, tn=_div_tile(3 * d, 1024, LANES),
                      tk=_div_tile(d, 2048, LANES), out_dtype=F32, bias=b.reshape(depth, 1, 3 * d),
                      silu_x=True, name="ada")
        return mod, jnp.repeat(mod[bp:n_cond], ts, axis=0)

    tm = _div_tile(m, MM_TM_PREF, 16)
    tr = _div_tile(ms, 256, 16)
    assert tp % tr == 0

    def residual(xs_in, w, l, x, mod, mod_s, name):
        tk, k_rem = _k_tile(xs_in[0].shape[1], 1536)
        resid = dict(res=x, mod=mod, gate_s=mod_s, gate_col=2 * d, bp=bp, tp=tp)
        return _matmul(xs_in, w, layer=l, n_out=d, tm=tm, tn=_div_tile(d, 512, LANES), tk=tk, k_rem=k_rem,
                       out_dtype=F32, resid=resid, name=name)

    gw = ev_ln_v_g.shape[-1]
    groups, gm_chunk = ev_w_s.shape[1], ev_w_s.shape[2]
    qr_rank = ev_g_qa.shape[-1]
    heads = ev_w_uk.shape[1]
    nope, rank = ev_w_uk.shape[2], ev_w_uk.shape[3]
    rope_d = ev_g_kr.shape[-1]
    qd = rank + rope_d
    ev_in = ev_w_in.shape[-1]
    cq_off, ckv_off, kr_off = 2 * gw, 2 * gw + qr_rank, 2 * gw + qr_rank + rank
    mla_scale = float(nope + rope_d) ** -0.5
    pos = jnp.concatenate([jnp.tile(jnp.arange(tp), bp), jnp.tile(past + jnp.arange(ts), bs)])
    cosf, sinf = _rope_tables(pos, rope_d)
    ev_w_in_t = jnp.swapaxes(ev_w_in, 1, 2)
    od_w_in_t = jnp.swapaxes(od_w_in, 1, 2)
    cache_krope_t = jnp.swapaxes(cache_krope, 2, 3)

    def mix_matrix(w_s, b_s, t):
        lc = min(t, gm_chunk)
        reps = gm_chunk // lc
        w = jnp.where(jnp.tril(jnp.ones((lc, lc), dtype=bool)), w_s[:, :lc, :lc], 0.0)
        w_blk = jnp.einsum("ab,gts->gatbs", jnp.eye(reps, dtype=F32), w).reshape(groups, gm_chunk, gm_chunk)
        b_blk = jnp.tile(b_s[:, :lc], (1, reps)).T
        return w_blk, b_blk

    ml_heads, ml_dk, ml_dv = state_mlstm_C.shape[2], state_mlstm_C.shape[3], state_mlstm_C.shape[4]
    lp = min(tp, ML_CHUNK)
    ncp = tp // lp
    ml_sub = ML_SAMPLE_ROWS // ts
    assert ML_SAMPLE_ROWS % ts == 0 and bs % ml_sub == 0 and mp % ML_SAMPLE_ROWS == 0

    x = jnp.concatenate([x_prompt.reshape(mp, d), x_sample.reshape(ms, d)], axis=0)
    ckv_p, kr_p, ckv_s, kr_s, v_s = [], [], [], [], []
    cp_l, np_l, mp_l, cs_l, ns_l, ms_l = [], [], [], [], [], []
    for l in range(depth):
        j = l // 2
        mod, mod_s = ada(ada_mix_w, ada_mix_b, l)
        h = _modulate(x, norm_mix_g.reshape(depth, 1, d), mod, mod_s, layer=l, tr=tr, bp=bp, tp=tp)
        if l % 2 == 0:
            z = _matmul([h], ev_w_in_t, layer=j, n_out=ev_in, tm=tm, tn=min(ev_in, 1024),
                        tk=_div_tile(d, 1024, LANES), w_t=True, out_dtype=F32, name="ev_in")
            w_p, b_p = mix_matrix(ev_w_s[j], ev_b_s[j], tp)
            w_smp, b_smp = mix_matrix(ev_w_s[j], ev_b_s[j], ts)
            a, v_ln = _gmlp(z, ev_ln_v_g.reshape(-1, 1, gw), ev_ln_v_b.reshape(-1, 1, gw),
                            jnp.stack([w_p, w_smp]), jnp.stack([b_p, b_smp]),
                            layer=j, gw=gw, n_prompt_blocks=mp // gm_chunk)
            tmq = _div_tile(ms, 512, 16)
            w_uq_h = ev_w_uq[j].reshape(qr_rank, heads, nope + rope_d).transpose(1, 0, 2)
            qcat = _mla_q(z, ev_g_qa.reshape(-1, 1, qr_rank), w_uq_h, ev_g_qk.reshape(-1, 1, nope + rope_d),
                          cosf, sinf, ev_w_uk, layer=j, cq_off=cq_off, tm=tmq, hb=_div_tile(heads, 4, 1),
                          scale=mla_scale)
            ckv_n, kr_n, kcat = _mla_kv(z, ev_g_kv.reshape(-1, 1, rank), ev_g_kr.reshape(-1, 1, rope_d),
                                        cosf, sinf, layer=j, ckv_off=ckv_off, kr_off=kr_off, tm=tmq)
            o = _prompt_attention(qcat, kcat, ev_w_uv, layer=j, bp=bp, tp=tp, tq=_div_tile(tp, ATTN_TQ_PREF, 16))
            q_s = qcat[:, mp:].reshape(heads, bs, ts, qd).transpose(1, 0, 2, 3).reshape(bs, heads * ts, qd)
            k_new = jnp.pad(kcat[mp:].reshape(bs, ts, qd), ((0, 0), (0, LANES - ts), (0, 0)))
            o_lat_s = _sample_attention(q_s, k_new, cache_ckv, cache_krope_t, page_table, layer=j, t_new=ts)
            o_lat_s = o_lat_s.reshape(bs, heads, ts, rank).transpose(1, 0, 2, 3).reshape(heads, ms, rank)
            o = _value_up(o_lat_s, ev_w_uv, o, layer=j, row0=mp)
            x = residual([a, o], ev_w_out, j, x, mod, mod_s, "ev_out")
            ckv_p.append(ckv_n[:mp].reshape(bp, tp, rank))
            kr_p.append(kr_n[:mp].reshape(bp, tp, rope_d))
            ckv_s.append(ckv_n[mp:].reshape(bs, ts, rank))
            kr_s.append(kr_n[mp:].reshape(bs, ts, rope_d))
            v_s.append(v_ln[mp:].reshape(bs, ts, gw))
        else:
            od_in = od_w_in.shape[-1]
            gate_off = od_in - 2 * ml_heads
            bg_c = od_b_gates.reshape(-1, 1, 2 * ml_heads)
            bg_r = od_b_gates.reshape(-1, 2 * ml_heads, 1)
            gmh = od_g_mh.reshape(-1, 1, ml_heads * ml_dv)
            z = _matmul([h], od_w_in_t, layer=j, n_out=od_in, tm=tm, tn=min(od_in, 1024),
                        tk=_div_tile(d, 1024, LANES), w_t=True, out_dtype=F32, name="od_in")
            gates = z[:, gate_off:]
            g_rows_p = jnp.swapaxes(gates[:mp].reshape(bp * ncp, lp, 2 * ml_heads), 1, 2)
            g_rows_s = jnp.swapaxes(gates[mp:].reshape(ms // ML_SAMPLE_ROWS, ML_SAMPLE_ROWS, 2 * ml_heads), 1, 2)
            zeros = lambda *shape: jnp.zeros(shape, F32)
            hm, c_p, n_p, m_p = _mlstm(z, g_rows_p, bg_c, bg_r, gmh, zeros(bp, ml_heads, ml_dk, ml_dv),
                                       zeros(bp, ml_heads, ml_dk), zeros(bp, ml_heads, 1), None,
                                       layer=j, nb=bp, nc=ncp, nsub=1, rows=lp, sub_rows=lp, row_blk0=0)
            hm, c_s, n_s, m_s = _mlstm(z, g_rows_s, bg_c, bg_r, gmh, state_mlstm_C[j], state_mlstm_n[j],
                                       state_mlstm_m[j].reshape(bs, ml_heads, 1), hm,
                                       layer=j, nb=bs // ml_sub, nc=1, nsub=ml_sub, rows=ML_SAMPLE_ROWS,
                                       sub_rows=ts, row_blk0=mp // ML_SAMPLE_ROWS)
            x = residual([hm], od_w_out, j, x, mod, mod_s, "od_out")
            cp_l.append(c_p)
            np_l.append(n_p)
            mp_l.append(m_p.reshape(bp, ml_heads))
            cs_l.append(c_s)
            ns_l.append(n_s)
            ms_l.append(m_s.reshape(bs, ml_heads))

        mod, mod_s = ada(ada_ffn_w, ada_ffn_b, l)
        h = _modulate(x, norm_ffn_g.reshape(depth, 1, d), mod, mod_s, layer=l, tr=tr, bp=bp, tp=tp)
        act = _swiglu_in(h, ffn_w_in, layer=l, tm=_div_tile(m, SWIGLU_TM_PREF, 16),
                         tk=_div_tile(d, 512, LANES), nsub=2)
        x = residual([act], ffn_w_out, l, x, mod, mod_s, "ffn_out")

    return (x[:mp].reshape(bp, tp, d), x[mp:].reshape(bs, ts, d),
            jnp.stack(ckv_p), jnp.stack(kr_p), jnp.stack(ckv_s), jnp.stack(kr_s), jnp.stack(v_s),
            jnp.stack(cp_l), jnp.stack(np_l), jnp.stack(mp_l),
            jnp.stack(cs_l), jnp.stack(ns_l), jnp.stack(ms_l))
```

```python
import functools

import jax
import jax.numpy as jnp
from jax import lax
from jax.experimental import pallas as pl
from jax.experimental.pallas import tpu as pltpu

EPS = 1e-6
ROPE_THETA = 10000.0
GATE_CAP = 15.0
ML_CHUNK = 64
ML_SAMPLE_ROWS = 16
PAGES_PER_STEP = 16
ATTN_RING_SLOTS = 3
LANES = 128
SUBLANES = 8
V7X_VMEM_LIMIT = 56 * 1024 * 1024
MM_TM_PREF = 2176
SWIGLU_TM_PREF = 4352
ATTN_TQ_PREF = 512
ATTN_HEADS_PER_STEP = 4

F32 = jnp.float32
BF16 = jnp.bfloat16


def _cparams(*sem):
    return pltpu.CompilerParams(dimension_semantics=sem, vmem_limit_bytes=V7X_VMEM_LIMIT)


def _div_tile(dim, pref, align):
    if dim <= pref:
        return dim
    t = (pref // align) * align
    while t >= align:
        if dim % t == 0:
            return t
        t -= align
    return dim


def _k_tile(kdim, pref):
    exact = _div_tile(kdim, pref, 256)
    if kdim <= pref or 2 * exact > pref:
        return exact, 0
    for tk in range((pref // 256) * 256, exact, -256):
        rem = kdim % tk
        if rem and (kdim - rem) % rem == 0 and rem % LANES == 0:
            return tk, rem
    return exact, 0


def _rms(x, g):
    return x * lax.rsqrt(jnp.mean(x * x, axis=-1, keepdims=True) + EPS) * g


def _gelu(x):
    return x * (0.5 * (1.0 + jnp.tanh(0.7978845608028654 * (x + 0.044715 * (x * x * x)))))


def _sigmoid(x):
    return 0.5 * (jnp.tanh(0.5 * x) + 1.0)


def _dot(a, b):
    return jnp.dot(a, b, preferred_element_type=F32)


def _dot_nt(a, b):
    return lax.dot_general(a, b, (((1,), (1,)), ((), ())), preferred_element_type=F32)


def _dot_tn(a, b):
    return lax.dot_general(a, b, (((0,), (0,)), ((), ())), preferred_element_type=F32)


def _skewed(n, stages):
    vals = [None] * n
    for t in range(n + len(stages) - 1):
        for s, fn in enumerate(stages):
            i = t - s
            if 0 <= i < n:
                vals[i] = fn(i, vals[i])
    return vals


def _row_segments(block_start, tm, n_prompt_rows, tp):
    segs, r, end = [], block_start, block_start + tm
    while r < end:
        if r < n_prompt_rows:
            seq = r // tp
            hi = min(end, (seq + 1) * tp)
            segs.append((r - block_start, hi - block_start, "p", seq))
        else:
            hi = end
            segs.append((r - block_start, hi - block_start, "s", r - n_prompt_rows))
        r = hi
    return segs


def _mm_kernel(*refs, nk, nx, nka, w_t, has_rem, silu_x, has_bias, segments):
    it = iter(refs)
    x_refs = [next(it) for _ in range(nx)]
    w_ref = next(it)
    xr_ref = next(it) if has_rem else None
    wr_ref = next(it) if has_rem else None
    b_ref = next(it) if has_bias else None
    res_ref = gp_ref = gs_ref = None
    if segments is not None:
        res_ref, gp_ref, gs_ref = next(it), next(it), next(it)
    o_ref = next(it)
    acc_ref = next(it)
    k = pl.program_id(2)

    def accum(x_ref, first):
        x = x_ref[...]
        if silu_x:
            x = x * _sigmoid(x)
        w = w_ref[...].astype(BF16)
        prod = _dot_nt(x.astype(BF16), w) if w_t else _dot(x.astype(BF16), w)
        if not first:
            acc_ref[...] += prod
        elif has_rem:
            acc_ref[...] = prod + _dot(xr_ref[...].astype(BF16), wr_ref[...].astype(BF16))
        else:
            acc_ref[...] = prod

    pl.when(k == 0)(lambda: accum(x_refs[0], True))
    if nx == 1:
        pl.when(k > 0)(lambda: accum(x_refs[0], False))
    else:
        pl.when((k > 0) & (k < nka))(lambda: accum(x_refs[0], False))
        pl.when(k >= nka)(lambda: accum(x_refs[1], False))

    @pl.when(k == nk - 1)
    def _finish():
        if segments is None:
            y = acc_ref[...]
            if has_bias:
                y = y + b_ref[...]
            o_ref[...] = y.astype(o_ref.dtype)
        else:
            i = pl.program_id(0)
            for bi, segs in enumerate(segments):
                @pl.when(i == bi)
                def _gated(segs=segs):
                    for lo, hi, kind, idx in segs:
                        gate = gp_ref[idx:idx + 1, :] if kind == "p" else gs_ref[idx:idx + hi - lo, :]
                        o_ref[lo:hi, :] = res_ref[lo:hi, :] + gate * acc_ref[lo:hi, :]


def _matmul(xs, w, *, layer, n_out, tm, tn, tk, out_dtype, w_t=False, k_rem=0, bias=None,
            silu_x=False, resid=None, name="mm"):
    m, k_each = xs[0].shape
    nx = len(xs)
    assert m % tm == 0 and (k_each - k_rem) % tk == 0 and (nx == 1 or k_rem == 0)
    nka = (k_each - k_rem) // tk
    nk = nka * nx
    grid = (m // tm, pl.cdiv(n_out, tn), nk)

    in_specs = [pl.BlockSpec((tm, tk), lambda i, j, k: (i, jnp.minimum(k, nka - 1)))]
    if nx == 2:
        in_specs.append(pl.BlockSpec((tm, tk), lambda i, j, k: (i, jnp.maximum(k - nka, 0))))
    if w_t:
        in_specs.append(pl.BlockSpec((None, tn, tk), lambda i, j, k: (layer, j, k)))
    else:
        in_specs.append(pl.BlockSpec((None, tk, tn), lambda i, j, k: (layer, k, j)))
    args = list(xs) + [w]
    if k_rem:
        assert not w_t
        rem_blk = (k_each - k_rem) // k_rem
        in_specs.append(pl.BlockSpec((tm, k_rem), lambda i, j, k: (i, rem_blk)))
        in_specs.append(pl.BlockSpec((None, k_rem, tn), lambda i, j, k: (layer, rem_blk, j)))
        args += [xs[0], w]
    if bias is not None:
        in_specs.append(pl.BlockSpec((None, 1, tn), lambda i, j, k: (layer, 0, j)))
        args.append(bias)
    segments = None
    if resid is not None:
        n_prompt_rows, tp = resid["bp"] * resid["tp"], resid["tp"]
        gate_blk = resid["gate_col"] // tn
        assert resid["gate_col"] % tn == 0 and resid["bp"] <= SUBLANES
        segments = [_row_segments(b * tm, tm, n_prompt_rows, tp) for b in range(m // tm)]
        ms = resid["gate_s"].shape[0]
        in_specs += [
            pl.BlockSpec((tm, tn), lambda i, j, k: (i, j)),
            pl.BlockSpec((SUBLANES, tn), lambda i, j, k: (0, gate_blk + j)),
            pl.BlockSpec((ms, tn), lambda i, j, k: (0, gate_blk + j)),
        ]
        args += [resid["res"], resid["mod"], resid["gate_s"]]
    kern = functools.partial(_mm_kernel, nk=nk, nx=nx, nka=nka, w_t=w_t, has_rem=bool(k_rem), silu_x=silu_x,
                             has_bias=bias is not None, segments=segments)
    return pl.pallas_call(
        kern,
        grid=grid,
        in_specs=in_specs,
        out_specs=pl.BlockSpec((tm, tn), lambda i, j, k: (i, j)),
        out_shape=jax.ShapeDtypeStruct((m, n_out), out_dtype),
        scratch_shapes=[pltpu.VMEM((tm, tn), F32)],
        compiler_params=_cparams("parallel", "parallel", "arbitrary"),
        name=name,
    )(*args)


SWIGLU_SUB = 256


def _swiglu_kernel(x_ref, *refs, nk, nsub):
    wg = refs[:nsub]
    wu = refs[nsub:2 * nsub]
    o_ref = refs[2 * nsub]
    accg, accu = refs[2 * nsub + 1:]
    k = pl.program_id(2)

    def accum(first):
        x = x_ref[...]
        for s in range(nsub):
            cols = slice(s * SWIGLU_SUB, (s + 1) * SWIGLU_SUB)
            pg = _dot(x, wg[s][...].astype(BF16))
            pu = _dot(x, wu[s][...].astype(BF16))
            if first:
                accg[:, cols] = pg
                accu[:, cols] = pu
            else:
                accg[:, cols] += pg
                accu[:, cols] += pu

    pl.when(k == 0)(lambda: accum(True))
    pl.when(k > 0)(lambda: accum(False))

    @pl.when(k == nk - 1)
    def _finish():
        g = accg[...]
        o_ref[...] = (g * _sigmoid(g) * accu[...]).astype(o_ref.dtype)


def _swiglu_in(x, w, *, layer, tm, tk, nsub):
    m, kdim = x.shape
    f = w.shape[2] // 2
    sub = SWIGLU_SUB
    assert f % sub == 0 and m % tm == 0 and kdim % tk == 0
    nfb = f // sub
    last = 2 * nfb - 1
    nk = kdim // tk
    tn = nsub * sub
    grid = (m // tm, pl.cdiv(f, tn), nk)
    in_specs = [pl.BlockSpec((tm, tk), lambda i, j, k: (i, k))]
    for s in range(nsub):
        in_specs.append(pl.BlockSpec((None, tk, sub), lambda i, j, k, s=s: (layer, k, j * nsub + s)))
    for s in range(nsub):
        in_specs.append(pl.BlockSpec(
            (None, tk, sub), lambda i, j, k, s=s: (layer, k, jnp.minimum(nfb + j * nsub + s, last))))
    return pl.pallas_call(
        functools.partial(_swiglu_kernel, nk=nk, nsub=nsub),
        grid=grid,
        in_specs=in_specs,
        out_specs=pl.BlockSpec((tm, tn), lambda i, j, k: (i, j)),
        out_shape=jax.ShapeDtypeStruct((m, f), BF16),
        scratch_shapes=[pltpu.VMEM((tm, tn), F32), pltpu.VMEM((tm, tn), F32)],
        compiler_params=_cparams("parallel", "parallel", "arbitrary"),
        name="swiglu_in",
    )(x, *([w] * (2 * nsub)))


def _modulate_kernel(x_ref, g_ref, shp_ref, scp_ref, shs_ref, scs_ref, o_ref, *, tr, tp, bp, npb):
    i = pl.program_id(0)
    seq = jnp.minimum(i * tr // tp, bp - 1)
    is_sample = i >= npb
    shift = jnp.where(is_sample, shs_ref[...], shp_ref[pl.ds(seq, 1), :])
    scale = jnp.where(is_sample, scs_ref[...], scp_ref[pl.ds(seq, 1), :])
    y = _rms(x_ref[...], g_ref[...])
    o_ref[...] = (y * (1.0 + scale) + shift).astype(o_ref.dtype)


def _modulate(x, g, mod, mod_s, *, layer, tr, bp, tp):
    m, d = x.shape
    npb = bp * tp // tr
    srow = lambda i: jnp.maximum(i - npb, 0)
    return pl.pallas_call(
        functools.partial(_modulate_kernel, tr=tr, tp=tp, bp=bp, npb=npb),
        grid=(m // tr,),
        in_specs=[
            pl.BlockSpec((tr, d), lambda i: (i, 0)),
            pl.BlockSpec((None, 1, d), lambda i: (layer, 0, 0)),
            pl.BlockSpec((SUBLANES, d), lambda i: (0, 0)),
            pl.BlockSpec((SUBLANES, d), lambda i: (0, 1)),
            pl.BlockSpec((tr, d), lambda i: (srow(i), 0)),
            pl.BlockSpec((tr, d), lambda i: (srow(i), 1)),
        ],
        out_specs=pl.BlockSpec((tr, d), lambda i: (i, 0)),
        out_shape=jax.ShapeDtypeStruct((m, d), BF16),
        compiler_params=_cparams("parallel"),
        name="modulate",
    )(x, g, mod, mod, mod_s, mod_s)


def _gmlp_kernel(u_ref, v_ref, lng_ref, lnb_ref, w_ref, b_ref, a_ref, vout_ref, *, groups, gdim):
    for g in range(groups):
        cols = slice(g * gdim, (g + 1) * gdim)
        vg = _gelu(v_ref[:, cols])
        mu = jnp.mean(vg, axis=-1, keepdims=True)
        xc = vg - mu
        var = jnp.mean(xc * xc, axis=-1, keepdims=True)
        y = xc * lax.rsqrt(var + EPS) * lng_ref[:, cols] + lnb_ref[:, cols]
        vout_ref[:, cols] = y
        mixed = _dot(w_ref[g].astype(BF16), y.astype(BF16)) + b_ref[:, g:g + 1]
        a_ref[:, cols] = (_gelu(u_ref[:, cols]) * mixed).astype(a_ref.dtype)


def _gmlp(z, ln_g, ln_b, w_mix, b_mix, *, layer, gw, n_prompt_blocks):
    m = z.shape[0]
    groups, rows = w_mix.shape[1], w_mix.shape[2]
    assert m % rows == 0
    which = lambda i: jnp.where(i < n_prompt_blocks, 0, 1)
    return pl.pallas_call(
        functools.partial(_gmlp_kernel, groups=groups, gdim=gw // groups),
        grid=(m // rows,),
        in_specs=[
            pl.BlockSpec((rows, gw), lambda i: (i, 0)),
            pl.BlockSpec((rows, gw), lambda i: (i, 1)),
            pl.BlockSpec((None, 1, gw), lambda i: (layer, 0, 0)),
            pl.BlockSpec((None, 1, gw), lambda i: (layer, 0, 0)),
            pl.BlockSpec((None, groups, rows, rows), lambda i: (which(i), 0, 0, 0)),
            pl.BlockSpec((None, rows, groups), lambda i: (which(i), 0, 0)),
        ],
        out_specs=[pl.BlockSpec((rows, gw), lambda i: (i, 0)), pl.BlockSpec((rows, gw), lambda i: (i, 0))],
        out_shape=[jax.ShapeDtypeStruct((m, gw), BF16), jax.ShapeDtypeStruct((m, gw), F32)],
        compiler_params=_cparams("parallel"),
        name="gmlp",
    )(z, z, ln_g, ln_b, w_mix, b_mix)


def _rope_rows(x, cosf, sinf):
    half = x.shape[-1] // 2
    swapped = jnp.concatenate([x[:, half:], x[:, :half]], axis=-1)
    return x * cosf + swapped * sinf


def _q_kernel(cq_ref, gqa_ref, wuq_ref, gqk_ref, cos_ref, sin_ref, wuk_ref, o_ref, cqn_sc,
              *, hb, nope, rank, scale):
    @pl.when(pl.program_id(1) == 0)
    def _norm():
        cqn_sc[...] = _rms(cq_ref[...], gqa_ref[...]).astype(BF16)

    for h in range(hb):
        q = _rms(_dot(cqn_sc[...], wuq_ref[h].astype(BF16)), gqk_ref[...])
        q_rope = _rope_rows(q[:, nope:], cos_ref[...], sin_ref[...])
        q_lat = _dot(q[:, :nope].astype(BF16), wuk_ref[h].astype(BF16))
        o_ref[h, :, :rank] = (q_lat * scale).astype(o_ref.dtype)
        o_ref[h, :, rank:] = (q_rope * scale).astype(o_ref.dtype)


def _mla_q(z, g_qa, w_uq_h, g_qk, cosf, sinf, w_uk, *, layer, cq_off, tm, hb, scale):
    m = z.shape[0]
    qr = g_qa.shape[-1]
    heads, _, qd = w_uq_h.shape
    nope, rank = w_uk.shape[2], w_uk.shape[3]
    rope_d = qd - nope
    assert cq_off % qr == 0 and heads % hb == 0
    cq_blk = cq_off // qr
    return pl.pallas_call(
        functools.partial(_q_kernel, hb=hb, nope=nope, rank=rank, scale=scale),
        grid=(m // tm, heads // hb),
        in_specs=[
            pl.BlockSpec((tm, qr), lambda i, h: (i, cq_blk)),
            pl.BlockSpec((None, 1, qr), lambda i, h: (layer, 0, 0)),
            pl.BlockSpec((hb, qr, qd), lambda i, h: (h, 0, 0)),
            pl.BlockSpec((None, 1, qd), lambda i, h: (layer, 0, 0)),
            pl.BlockSpec((tm, rope_d), lambda i, h: (i, 0)),
            pl.BlockSpec((tm, rope_d), lambda i, h: (i, 0)),
            pl.BlockSpec((None, hb, nope, rank), lambda i, h: (layer, h, 0, 0)),
        ],
        out_specs=pl.BlockSpec((hb, tm, rank + rope_d), lambda i, h: (h, i, 0)),
        out_shape=jax.ShapeDtypeStruct((heads, m, rank + rope_d), BF16),
        scratch_shapes=[pltpu.VMEM((tm, qr), BF16)],
        compiler_params=_cparams("parallel", "arbitrary"),
        name="mla_q",
    )(z, g_qa, w_uq_h, g_qk, cosf, sinf, w_uk)


def _kv_kernel(ckv_ref, kr_ref, gkv_ref, gkr_ref, cos_ref, sin_ref, ckv_o, kr_o, kcat_o,
               *, rank, rope_d):
    c = _rms(ckv_ref[...], gkv_ref[...])
    r = _rope_rows(_rms(kr_ref[:, :rope_d], gkr_ref[...]), cos_ref[...], sin_ref[...])
    ckv_o[...] = c
    kr_o[...] = r
    kcat_o[:, :rank] = c.astype(kcat_o.dtype)
    kcat_o[:, rank:] = r.astype(kcat_o.dtype)


def _mla_kv(z, g_kv, g_kr, cosf, sinf, *, layer, ckv_off, kr_off, tm):
    m = z.shape[0]
    rank = g_kv.shape[-1]
    rope_d = g_kr.shape[-1]
    assert ckv_off % rank == 0 and kr_off % LANES == 0
    return pl.pallas_call(
        functools.partial(_kv_kernel, rank=rank, rope_d=rope_d),
        grid=(m // tm,),
        in_specs=[
            pl.BlockSpec((tm, rank), lambda i: (i, ckv_off // rank)),
            pl.BlockSpec((tm, LANES), lambda i: (i, kr_off // LANES)),
            pl.BlockSpec((None, 1, rank), lambda i: (layer, 0, 0)),
            pl.BlockSpec((None, 1, rope_d), lambda i: (layer, 0, 0)),
            pl.BlockSpec((tm, rope_d), lambda i: (i, 0)),
            pl.BlockSpec((tm, rope_d), lambda i: (i, 0)),
        ],
        out_specs=[
            pl.BlockSpec((tm, rank), lambda i: (i, 0)),
            pl.BlockSpec((tm, rope_d), lambda i: (i, 0)),
            pl.BlockSpec((tm, rank + rope_d), lambda i: (i, 0)),
        ],
        out_shape=[
            jax.ShapeDtypeStruct((m, rank), F32),
            jax.ShapeDtypeStruct((m, rope_d), F32),
            jax.ShapeDtypeStruct((m, rank + rope_d), BF16),
        ],
        compiler_params=_cparams("parallel"),
        name="mla_kv",
    )(z, z, g_kv, g_kr, cosf, sinf)


def _pattn_kernel(q_ref, k_ref, wuv_ref, prev_ref, o_ref, m_sc, l_sc, acc_sc, *, heads, tq, rank, vd):
    del prev_ref
    qi = pl.program_id(2)

    m_sc[...] = jnp.full_like(m_sc, -jnp.inf)
    l_sc[...] = jnp.zeros_like(l_sc)
    acc_sc[...] = jnp.zeros_like(acc_sc)

    def step(kb, masked):
        kblk = k_ref[pl.ds(pl.multiple_of(kb * tq, tq), tq), :]
        vblk = kblk[:, :rank]
        if masked:
            visible = (lax.broadcasted_iota(jnp.int32, (tq, tq), 1)
                       <= lax.broadcasted_iota(jnp.int32, (tq, tq), 0))
        def scores(h, _):
            s = _dot_nt(q_ref[h], kblk)
            return jnp.where(visible, s, -jnp.inf) if masked else s

        def softmax(h, s):
            rs = slice(h * tq, (h + 1) * tq)
            m_prev = m_sc[rs, :]
            m_new = jnp.maximum(m_prev, jnp.max(s, axis=-1, keepdims=True))
            alpha = jnp.exp(m_prev - m_new)
            p = jnp.exp(s - m_new)
            l_sc[rs, :] = alpha * l_sc[rs, :] + jnp.sum(p, axis=-1, keepdims=True)
            m_sc[rs, :] = m_new
            return alpha, p.astype(BF16)

        def values(h, ap):
            rs = slice(h * tq, (h + 1) * tq)
            acc_sc[rs, :] = ap[0] * acc_sc[rs, :] + _dot(ap[1], vblk)

        _skewed(heads, [scores, softmax, values])

    def body(kb, carry):
        step(kb, False)
        return carry

    lax.fori_loop(0, qi, body, 0)
    step(qi, True)
    for h in range(heads):
        rs = slice(h * tq, (h + 1) * tq)
        o_lat = (acc_sc[rs, :] / l_sc[rs, :]).astype(BF16)
        o_ref[:, h * vd:(h + 1) * vd] = _dot(o_lat, wuv_ref[h].astype(BF16)).astype(o_ref.dtype)


def _prompt_attention(qcat, kcat, w_uv, *, layer, bp, tp, tq):
    heads, m, qd = qcat.shape
    rank, vd = w_uv.shape[2], w_uv.shape[3]
    nq = tp // tq
    hg = _div_tile(heads, ATTN_HEADS_PER_STEP, 1)
    return pl.pallas_call(
        functools.partial(_pattn_kernel, heads=hg, tq=tq, rank=rank, vd=vd),
        grid=(bp, heads // hg, nq),
        in_specs=[
            pl.BlockSpec((hg, tq, qd), lambda b, g, qi: (g, b * nq + qi, 0)),
            pl.BlockSpec((tp, qd), lambda b, g, qi: (b, 0)),
            pl.BlockSpec((None, hg, rank, vd), lambda b, g, qi: (layer, g, 0, 0)),
            pl.BlockSpec(memory_space=pl.ANY),
        ],
        out_specs=pl.BlockSpec((tq, hg * vd), lambda b, g, qi: (b * nq + qi, g)),
        out_shape=jax.ShapeDtypeStruct((m, heads * vd), BF16),
        input_output_aliases={3: 0},
        scratch_shapes=[pltpu.VMEM((hg * tq, 1), F32), pltpu.VMEM((hg * tq, 1), F32),
                        pltpu.VMEM((hg * tq, rank), F32)],
        compiler_params=_cparams("parallel", "parallel", "arbitrary"),
        name="prompt_attn",
    )(qcat, kcat, w_uv, jnp.zeros((m, heads * vd), BF16))


def _sattn_kernel(pt_ref, q_ref, knew_ref, ckv_hbm, krt_hbm, o_ref, ckv_ring, krt_ring, kbuf, krt_buf, sems,
                  m_sc, l_sc, acc_sc, *, layer, pb, page, rank, nchunks, total, t_new):
    b = pl.program_id(0)
    slots = ATTN_RING_SLOTS

    def chunk_copies(page_ids, slot):
        copies = []
        for i, pid in enumerate(page_ids):
            copies.append(pltpu.make_async_copy(
                ckv_hbm.at[layer, pid], ckv_ring.at[slot, pl.ds(i * page, page), :], sems.at[slot]))
            copies.append(pltpu.make_async_copy(krt_hbm.at[layer, pid], krt_ring.at[slot, i], sems.at[slot]))
        return copies

    def start_chunk(g):
        src = lax.rem(g, total)
        for cp in chunk_copies([pt_ref[src * pb + i] for i in range(pb)], lax.rem(g, slots)):
            cp.start()

    def wait_chunk(g):
        for cp in chunk_copies([0] * pb, lax.rem(g, slots)):
            cp.wait()

    @pl.when(b == 0)
    def _prime():
        start_chunk(0)
        start_chunk(1)

    m_sc[...] = jnp.full_like(m_sc, -jnp.inf)
    l_sc[...] = jnp.zeros_like(l_sc)
    acc_sc[...] = jnp.zeros_like(acc_sc)
    q = q_ref[...]

    def online_update(s, values):
        m_prev = m_sc[...]
        m_new = jnp.maximum(m_prev, jnp.max(s, axis=-1, keepdims=True))
        alpha = jnp.exp(m_prev - m_new)
        p = jnp.exp(s - m_new)
        l_sc[...] = alpha * l_sc[...] + jnp.sum(p, axis=-1, keepdims=True)
        acc_sc[...] = alpha * acc_sc[...] + _dot(p.astype(BF16), values)
        m_sc[...] = m_new

    def chunk(c, carry):
        g = b * nchunks + c
        slot = lax.rem(g, slots)
        wait_chunk(g)
        start_chunk(g + 2)
        kbuf[...] = ckv_ring[slot].astype(BF16)
        for i in range(pb):
            krt_buf[:, i * page:(i + 1) * page] = krt_ring[slot, i].astype(BF16)
        s = _dot_nt(q[:, :rank], kbuf[...]) + _dot(q[:, rank:], krt_buf[...])
        online_update(s, kbuf[...])
        return carry

    lax.fori_loop(0, nchunks, chunk, 0)

    k_new = knew_ref[...]
    s_new = _dot_nt(q, k_new)
    row = lax.rem(lax.broadcasted_iota(jnp.int32, s_new.shape, 0), t_new)
    col = lax.broadcasted_iota(jnp.int32, s_new.shape, 1)
    online_update(jnp.where(col <= row, s_new, -jnp.inf), k_new[:, :rank])
    o_ref[...] = (acc_sc[...] / l_sc[...]).astype(o_ref.dtype)

    @pl.when(b == pl.num_programs(0) - 1)
    def _drain():
        wait_chunk(total)
        wait_chunk(total + 1)


def _sample_attention(q_s, k_new, cache_ckv, cache_krope_t, page_table, *, layer, t_new):
    b, rows, qd = q_s.shape
    page, rank = cache_ckv.shape[2], cache_ckv.shape[3]
    rope_d = cache_krope_t.shape[2]
    n_pages = page_table.shape[1]
    pb = _div_tile(n_pages, PAGES_PER_STEP, 1)
    nchunks = n_pages // pb
    total = b * nchunks
    assert total >= 2
    grid_spec = pltpu.PrefetchScalarGridSpec(
        num_scalar_prefetch=1,
        grid=(b,),
        in_specs=[
            pl.BlockSpec((None, rows, qd), lambda bi, pt: (bi, 0, 0)),
            pl.BlockSpec((None, k_new.shape[1], qd), lambda bi, pt: (bi, 0, 0)),
            pl.BlockSpec(memory_space=pl.ANY),
            pl.BlockSpec(memory_space=pl.ANY),
        ],
        out_specs=pl.BlockSpec((None, rows, rank), lambda bi, pt: (bi, 0, 0)),
        scratch_shapes=[pltpu.VMEM((ATTN_RING_SLOTS, pb * page, rank), cache_ckv.dtype),
                        pltpu.VMEM((ATTN_RING_SLOTS, pb, rope_d, page), cache_krope_t.dtype),
                        pltpu.VMEM((pb * page, rank), BF16), pltpu.VMEM((rope_d, pb * page), BF16),
                        pltpu.SemaphoreType.DMA((ATTN_RING_SLOTS,)),
                        pltpu.VMEM((rows, 1), F32), pltpu.VMEM((rows, 1), F32),
                        pltpu.VMEM((rows, rank), F32)],
    )
    return pl.pallas_call(
        functools.partial(_sattn_kernel, layer=layer, pb=pb, page=page, rank=rank, nchunks=nchunks,
                          total=total, t_new=t_new),
        grid_spec=grid_spec,
        out_shape=jax.ShapeDtypeStruct((b, rows, rank), BF16),
        compiler_params=_cparams("arbitrary"),
        name="sample_attn",
    )(page_table.reshape(-1), q_s, k_new, cache_ckv, cache_krope_t)


def _uv_kernel(x_ref, w_ref, prev_ref, o_ref):
    del prev_ref
    o_ref[...] = _dot(x_ref[...], w_ref[...].astype(BF16)).astype(o_ref.dtype)


def _value_up(o_lat, w_uv, o_prev, *, layer, row0):
    heads, ms, rank = o_lat.shape
    vd = w_uv.shape[3]
    assert row0 % ms == 0
    return pl.pallas_call(
        _uv_kernel,
        grid=(heads,),
        in_specs=[
            pl.BlockSpec((None, ms, rank), lambda h: (h, 0, 0)),
            pl.BlockSpec((None, None, rank, vd), lambda h: (layer, h, 0, 0)),
            pl.BlockSpec(memory_space=pl.ANY),
        ],
        out_specs=pl.BlockSpec((ms, vd), lambda h: (row0 // ms, h)),
        out_shape=jax.ShapeDtypeStruct(o_prev.shape, o_prev.dtype),
        input_output_aliases={2: 0},
        compiler_params=_cparams("parallel"),
        name="value_up",
    )(o_lat, w_uv, o_prev)


def _mlstm_kernel(*refs, heads, dk, dv, rows, sub_rows, nsub, aliased):
    (q_ref, k_ref, v_ref, o_ref, gc_ref, gr_ref, bgc_ref, bgr_ref, gmh_ref, c0_ref, n0_ref, m0_ref) = refs[:12]
    h_ref, c_ref, n_ref, m_ref = refs[-4:]
    sub = pl.program_id(2)

    @pl.when(pl.program_id(1) == 0)
    def _load_state():
        c_ref[...] = c0_ref[...]
        n_ref[...] = n0_ref[...]
        m_ref[...] = m0_ref[...]

    if nsub > 1:
        @pl.when(sub == 0)
        def _clear_rows():
            h_ref[...] = jnp.zeros_like(h_ref)

    n_all = n_ref[...]
    m_all = m_ref[...]
    n_out = n_all
    m_out = m_all
    head_id = lax.broadcasted_iota(jnp.int32, (heads, 1), 0)

    g_col = GATE_CAP * jnp.tanh((gc_ref[:, :2 * heads] + bgc_ref[...]) / GATE_CAP)
    g_row = GATE_CAP * jnp.tanh((gr_ref[...] + bgr_ref[...]) / GATE_CAP)

    def log_sigmoid(x):
        return jnp.minimum(x, 0.0) - jnp.log(1.0 + jnp.exp(-jnp.abs(x)))

    li = lax.broadcasted_iota(jnp.int32, (rows, rows), 0)
    si = lax.broadcasted_iota(jnp.int32, (rows, rows), 1)
    causal = si <= li
    ri = lax.broadcasted_iota(jnp.int32, (rows, 1), 0)
    ci = lax.broadcasted_iota(jnp.int32, (1, rows), 1)
    lo, hi = sub * sub_rows, (sub + 1) * sub_rows
    row_ok = (ri >= lo) & (ri < hi)
    col_ok = (ci >= lo) & (ci < hi)
    k_scale = dk ** -0.5

    state_rows = []

    def gates(h, _):
        ig_c = jnp.where(row_ok, g_col[:, h:h + 1], -jnp.inf)
        lf_c = jnp.where(row_ok, log_sigmoid(g_col[:, heads + h:heads + h + 1]), 0.0)
        ig_r = jnp.where(col_ok, g_row[h:h + 1, :], -jnp.inf)
        lf_r = jnp.where(col_ok, log_sigmoid(g_row[heads + h:heads + h + 1, :]), 0.0)
        b_c = jnp.sum(jnp.where(causal, lf_r, 0.0), axis=1, keepdims=True)
        b_r = jnp.sum(jnp.where(li <= si, lf_c, 0.0), axis=0, keepdims=True)
        b_last = jnp.sum(lf_r, axis=1, keepdims=True)
        m0 = m_all[h:h + 1, :]
        dmat = jnp.where(causal, b_c - b_r + ig_r, -jnp.inf)
        inter = b_c + m0
        m_pos = jnp.maximum(inter, jnp.max(dmat, axis=1, keepdims=True))
        g_r = b_last - b_r + ig_r
        m_new = jnp.maximum(b_last + m0, jnp.max(g_r, axis=1, keepdims=True))
        return dict(decay_w=jnp.exp(dmat - m_pos), a=jnp.exp(inter - m_pos), floor=jnp.exp(-m_pos),
                    m_new=m_new, decay=jnp.exp(b_last + m0 - m_new), wk_c=jnp.exp(b_last - b_c + ig_c - m_new))

    def products(h, st):
        qb = q_ref[:, h * dk:(h + 1) * dk].astype(BF16)
        kh = k_ref[:, h * dk:(h + 1) * dk] * k_scale
        st["qk"] = _dot_nt(qb, kh.astype(BF16))
        st["qc"] = _dot(qb, c_ref[h].astype(BF16))
        return st

    def update(h, st):
        qh = q_ref[:, h * dk:(h + 1) * dk]
        kh = k_ref[:, h * dk:(h + 1) * dk] * k_scale
        vb = v_ref[:, h * dv:(h + 1) * dv].astype(BF16)
        n0 = n_all[h:h + 1, :]
        w = st["decay_w"] * st["qk"]
        st["num"] = st["a"] * st["qc"] + _dot(w.astype(BF16), vb)
        st["nq"] = st["a"] * jnp.sum(qh * n0, axis=1, keepdims=True) + jnp.sum(w, axis=1, keepdims=True)
        kw = st["wk_c"] * kh
        c_ref[h] = st["decay"] * c_ref[h] + _dot_tn(kw.astype(BF16), vb)
        state_rows.append((st["decay"] * n0 + jnp.sum(kw, axis=0, keepdims=True), st["m_new"]))
        return st

    def emit(h, st):
        hs = st["num"] / jnp.maximum(jnp.abs(st["nq"]), st["floor"])
        hn = _rms(hs, gmh_ref[:, h * dv:(h + 1) * dv])
        out = (_sigmoid(o_ref[:, h * dv:(h + 1) * dv]) * hn).astype(h_ref.dtype)
        cols = slice(h * dv, (h + 1) * dv)
        if nsub == 1:
            h_ref[:, cols] = out
        else:
            h_ref[:, cols] = jnp.where(row_ok, out, h_ref[:, cols])

    _skewed(heads, [gates, products, update, emit])
    for h, (n_new, m_new) in enumerate(state_rows):
        n_out = jnp.where(head_id == h, n_new, n_out)
        m_out = jnp.where(head_id == h, m_new, m_out)
    n_ref[...] = n_out
    m_ref[...] = m_out


def _mlstm(z, g_rows, b_gates_c, b_gates_r, g_mh, c0, n0, m0, h_prev, *, layer, nb, nc, nsub,
           rows, sub_rows, row_blk0):
    m = z.shape[0]
    _, heads, dk, dv = c0.shape
    qw, vw = heads * dk, heads * dv
    assert (2 * qw) % vw == 0 and (2 * qw + 2 * vw) % LANES == 0
    g_blk = (2 * qw + 2 * vw) // LANES
    blk = lambda b, c, s: row_blk0 + b * nc + c
    seq = lambda b, c, s: b * nsub + s
    in_specs = [
        pl.BlockSpec((rows, qw), lambda b, c, s: (blk(b, c, s), 0)),
        pl.BlockSpec((rows, qw), lambda b, c, s: (blk(b, c, s), 1)),
        pl.BlockSpec((rows, vw), lambda b, c, s: (blk(b, c, s), 2 * qw // vw)),
        pl.BlockSpec((rows, vw), lambda b, c, s: (blk(b, c, s), 2 * qw // vw + 1)),
        pl.BlockSpec((rows, LANES), lambda b, c, s: (blk(b, c, s), g_blk)),
        pl.BlockSpec((None, 2 * heads, rows), lambda b, c, s: (b * nc + c, 0, 0)),
        pl.BlockSpec((None, 1, 2 * heads), lambda b, c, s: (layer, 0, 0)),
        pl.BlockSpec((None, 2 * heads, 1), lambda b, c, s: (layer, 0, 0)),
        pl.BlockSpec((None, 1, vw), lambda b, c, s: (layer, 0, 0)),
        pl.BlockSpec((None, heads, dk, dv), lambda b, c, s: (seq(b, c, s), 0, 0, 0)),
        pl.BlockSpec((None, heads, dk), lambda b, c, s: (seq(b, c, s), 0, 0)),
        pl.BlockSpec((None, heads, 1), lambda b, c, s: (seq(b, c, s), 0, 0)),
    ]
    args = [z, z, z, z, z, g_rows, b_gates_c, b_gates_r, g_mh, c0, n0, m0]
    aliases = {}
    if h_prev is not None:
        in_specs.append(pl.BlockSpec(memory_space=pl.ANY))
        args.append(h_prev)
        aliases = {12: 0}
    return pl.pallas_call(
        functools.partial(_mlstm_kernel, heads=heads, dk=dk, dv=dv, rows=rows, sub_rows=sub_rows,
                          nsub=nsub, aliased=h_prev is not None),
        grid=(nb, nc, nsub),
        in_specs=in_specs,
        out_specs=[
            pl.BlockSpec((rows, vw), lambda b, c, s: (blk(b, c, s), 0)),
            pl.BlockSpec((None, heads, dk, dv), lambda b, c, s: (seq(b, c, s), 0, 0, 0)),
            pl.BlockSpec((None, heads, dk), lambda b, c, s: (seq(b, c, s), 0, 0)),
            pl.BlockSpec((None, heads, 1), lambda b, c, s: (seq(b, c, s), 0, 0)),
        ],
        out_shape=[
            jax.ShapeDtypeStruct((m, vw), BF16),
            jax.ShapeDtypeStruct((nb * nsub, heads, dk, dv), F32),
            jax.ShapeDtypeStruct((nb * nsub, heads, dk), F32),
            jax.ShapeDtypeStruct((nb * nsub, heads, 1), F32),
        ],
        input_output_aliases=aliases,
        compiler_params=_cparams("parallel", "arbitrary", "arbitrary"),
        name="mlstm",
    )(*args)


def _rope_tables(pos, rope_d):
    half = rope_d // 2
    inv = ROPE_THETA ** (-jnp.arange(half, dtype=F32) * (2.0 / rope_d))
    ang = pos.astype(F32)[:, None] * inv[None, :]
    cos, sin = jnp.cos(ang), jnp.sin(ang)
    return jnp.concatenate([cos, cos], axis=-1), jnp.concatenate([-sin, sin], axis=-1)


def kernel(x_prompt, x_sample, c_prompt, c_sample, cache_ckv, cache_krope, page_table,
           state_mlstm_C, state_mlstm_n, state_mlstm_m,
           norm_mix_g, ada_mix_w, ada_mix_b, norm_ffn_g, ada_ffn_w, ada_ffn_b, ffn_w_in, ffn_w_out,
           ev_w_in, ev_ln_v_g, ev_ln_v_b, ev_w_s, ev_b_s, ev_g_qa, ev_w_uq, ev_g_qk, ev_g_kv, ev_g_kr,
           ev_w_uk, ev_w_uv, ev_w_out, od_w_in, od_b_gates, od_g_mh, od_w_out):
    bp, tp, d = x_prompt.shape
    bs, ts, _ = x_sample.shape
    depth = norm_mix_g.shape[0]
    mp, ms = bp * tp, bs * ts
    m = mp + ms
    past = page_table.shape[1] * cache_ckv.shape[2]

    n_cond = bp + bs
    cond_rows = -(-n_cond // 16) * 16
    c_all = jnp.concatenate([c_prompt, c_sample, jnp.zeros((cond_rows - n_cond, d), F32)], axis=0)

    def ada(w, b, l):
        mod = _matmul([c_all], w, layer=l, n_out=3 * d, tm=cond_rows, tn=_div_tile(3 * d, 1024, LANES),
                      tk=_div_tile(d, 2048, LANES), out_dtype=F32, bias=b.reshape(depth, 1, 3 * d),
                      silu_x=True, name="ada")
        return mod, jnp.repeat(mod[bp:n_cond], ts, axis=0)

    tm = _div_tile(m, MM_TM_PREF, 16)
    tr = _div_tile(ms, 256, 16)
    assert tp % tr == 0

    def residual(xs_in, w, l, x, mod, mod_s, name):
        tk, k_rem = _k_tile(xs_in[0].shape[1], 1536)
        resid = dict(res=x, mod=mod, gate_s=mod_s, gate_col=2 * d, bp=bp, tp=tp)
        return _matmul(xs_in, w, layer=l, n_out=d, tm=tm, tn=_div_tile(d, 512, LANES), tk=tk, k_rem=k_rem,
                       out_dtype=F32, resid=resid, name=name)

    gw = ev_ln_v_g.shape[-1]
    groups, gm_chunk = ev_w_s.shape[1], ev_w_s.shape[2]
    qr_rank = ev_g_qa.shape[-1]
    heads = ev_w_uk.shape[1]
    nope, rank = ev_w_uk.shape[2], ev_w_uk.shape[3]
    rope_d = ev_g_kr.shape[-1]
    qd = rank + rope_d
    ev_in = ev_w_in.shape[-1]
    cq_off, ckv_off, kr_off = 2 * gw, 2 * gw + qr_rank, 2 * gw + qr_rank + rank
    mla_scale = float(nope + rope_d) ** -0.5
    pos = jnp.concatenate([jnp.tile(jnp.arange(tp), bp), jnp.tile(past + jnp.arange(ts), bs)])
    cosf, sinf = _rope_tables(pos, rope_d)
    ev_w_in_t = jnp.swapaxes(ev_w_in, 1, 2)
    od_w_in_t = jnp.swapaxes(od_w_in, 1, 2)
    cache_krope_t = jnp.swapaxes(cache_krope, 2, 3)

    def mix_matrix(w_s, b_s, t):
        lc = min(t, gm_chunk)
        reps = gm_chunk // lc
        w = jnp.where(jnp.tril(jnp.ones((lc, lc), dtype=bool)), w_s[:, :lc, :lc], 0.0)
        w_blk = jnp.einsum("ab,gts->gatbs", jnp.eye(reps, dtype=F32), w).reshape(groups, gm_chunk, gm_chunk)
        b_blk = jnp.tile(b_s[:, :lc], (1, reps)).T
        return w_blk, b_blk

    ml_heads, ml_dk, ml_dv = state_mlstm_C.shape[2], state_mlstm_C.shape[3], state_mlstm_C.shape[4]
    lp = min(tp, ML_CHUNK)
    ncp = tp // lp
    ml_sub = ML_SAMPLE_ROWS // ts
    assert ML_SAMPLE_ROWS % ts == 0 and bs % ml_sub == 0 and mp % ML_SAMPLE_ROWS == 0

    x = jnp.concatenate([x_prompt.reshape(mp, d), x_sample.reshape(ms, d)], axis=0)
    ckv_p, kr_p, ckv_s, kr_s, v_s = [], [], [], [], []
    cp_l, np_l, mp_l, cs_l, ns_l, ms_l = [], [], [], [], [], []
    for l in range(depth):
        j = l // 2
        mod, mod_s = ada(ada_mix_w, ada_mix_b, l)
        h = _modulate(x, norm_mix_g.reshape(depth, 1, d), mod, mod_s, layer=l, tr=tr, bp=bp, tp=tp)
        if l % 2 == 0:
            z = _matmul([h], ev_w_in_t, layer=j, n_out=ev_in, tm=tm, tn=min(ev_in, 1024),
                        tk=_div_tile(d, 1024, LANES), w_t=True, out_dtype=F32, name="ev_in")
            w_p, b_p = mix_matrix(ev_w_s[j], ev_b_s[j], tp)
            w_smp, b_smp = mix_matrix(ev_w_s[j], ev_b_s[j], ts)
            a, v_ln = _gmlp(z, ev_ln_v_g.reshape(-1, 1, gw), ev_ln_v_b.reshape(-1, 1, gw),
                            jnp.stack([w_p, w_smp]), jnp.stack([b_p, b_smp]),
                            layer=j, gw=gw, n_prompt_blocks=mp // gm_chunk)
            tmq = _div_tile(ms, 512, 16)
            w_uq_h = ev_w_uq[j].reshape(qr_rank, heads, nope + rope_d).transpose(1, 0, 2)
            qcat = _mla_q(z, ev_g_qa.reshape(-1, 1, qr_rank), w_uq_h, ev_g_qk.reshape(-1, 1, nope + rope_d),
                          cosf, sinf, ev_w_uk, layer=j, cq_off=cq_off, tm=tmq, hb=_div_tile(heads, 4, 1),
                          scale=mla_scale)
            ckv_n, kr_n, kcat = _mla_kv(z, ev_g_kv.reshape(-1, 1, rank), ev_g_kr.reshape(-1, 1, rope_d),
                                        cosf, sinf, layer=j, ckv_off=ckv_off, kr_off=kr_off, tm=tmq)
            o = _prompt_attention(qcat, kcat, ev_w_uv, layer=j, bp=bp, tp=tp, tq=_div_tile(tp, ATTN_TQ_PREF, 16))
            q_s = qcat[:, mp:].reshape(heads, bs, ts, qd).transpose(1, 0, 2, 3).reshape(bs, heads * ts, qd)
            k_new = jnp.pad(kcat[mp:].reshape(bs, ts, qd), ((0, 0), (0, LANES - ts), (0, 0)))
            o_lat_s = _sample_attention(q_s, k_new, cache_ckv, cache_krope_t, page_table, layer=j, t_new=ts)
            o_lat_s = o_lat_s.reshape(bs, heads, ts, rank).transpose(1, 0, 2, 3).reshape(heads, ms, rank)
            o = _value_up(o_lat_s, ev_w_uv, o, layer=j, row0=mp)
            x = residual([a, o], ev_w_out, j, x, mod, mod_s, "ev_out")
            ckv_p.append(ckv_n[:mp].reshape(bp, tp, rank))
            kr_p.append(kr_n[:mp].reshape(bp, tp, rope_d))
            ckv_s.append(ckv_n[mp:].reshape(bs, ts, rank))
            kr_s.append(kr_n[mp:].reshape(bs, ts, rope_d))
            v_s.append(v_ln[mp:].reshape(bs, ts, gw))
        else:
            od_in = od_w_in.shape[-1]
            gate_off = od_in - 2 * ml_heads
            bg_c = od_b_gates.reshape(-1, 1, 2 * ml_heads)
            bg_r = od_b_gates.reshape(-1, 2 * ml_heads, 1)
            gmh = od_g_mh.reshape(-1, 1, ml_heads * ml_dv)
            z = _matmul([h], od_w_in_t, layer=j, n_out=od_in, tm=tm, tn=min(od_in, 1024),
                        tk=_div_tile(d, 1024, LANES), w_t=True, out_dtype=F32, name="od_in")
            gates = z[:, gate_off:]
            g_rows_p = jnp.swapaxes(gates[:mp].reshape(bp * ncp, lp, 2 * ml_heads), 1, 2)
            g_rows_s = jnp.swapaxes(gates[mp:].reshape(ms // ML_SAMPLE_ROWS, ML_SAMPLE_ROWS, 2 * ml_heads), 1, 2)
            zeros = lambda *shape: jnp.zeros(shape, F32)
            hm, c_p, n_p, m_p = _mlstm(z, g_rows_p, bg_c, bg_r, gmh, zeros(bp, ml_heads, ml_dk, ml_dv),
                                       zeros(bp, ml_heads, ml_dk), zeros(bp, ml_heads, 1),
                                       jnp.zeros((m, ml_heads * ml_dv), BF16),
                                       layer=j, nb=bp, nc=ncp, nsub=1, rows=lp, sub_rows=lp, row_blk0=0)
            hm, c_s, n_s, m_s = _mlstm(z, g_rows_s, bg_c, bg_r, gmh, state_mlstm_C[j], state_mlstm_n[j],
                                       state_mlstm_m[j].reshape(bs, ml_heads, 1), hm,
                                       layer=j, nb=bs // ml_sub, nc=1, nsub=ml_sub, rows=ML_SAMPLE_ROWS,
                                       sub_rows=ts, row_blk0=mp // ML_SAMPLE_ROWS)
            x = residual([hm], od_w_out, j, x, mod, mod_s, "od_out")
            cp_l.append(c_p)
            np_l.append(n_p)
            mp_l.append(m_p.reshape(bp, ml_heads))
            cs_l.append(c_s)
            ns_l.append(n_s)
            ms_l.append(m_s.reshape(bs, ml_heads))

        mod, mod_s = ada(ada_ffn_w, ada_ffn_b, l)
        h = _modulate(x, norm_ffn_g.reshape(depth, 1, d), mod, mod_s, layer=l, tr=tr, bp=bp, tp=tp)
        act = _swiglu_in(h, ffn_w_in, layer=l, tm=_div_tile(m, SWIGLU_TM_PREF, 16),
                         tk=_div_tile(d, 512, LANES), nsub=2)
        x = residual([act], ffn_w_out, l, x, mod, mod_s, "ffn_out")

    return (x[:mp].reshape(bp, tp, d), x[mp:].reshape(bs, ts, d),
            jnp.stack(ckv_p), jnp.stack(kr_p), jnp.stack(ckv_s), jnp.stack(kr_s), jnp.stack(v_s),
            jnp.stack(cp_l), jnp.stack(np_l), jnp.stack(mp_l),
            jnp.stack(cs_l), jnp.stack(ns_l), jnp.stack(ms_l))
```

```python
import functools

import jax
import jax.numpy as jnp
from jax import lax
from jax.experimental import pallas as pl
from jax.experimental.pallas import tpu as pltpu

EPS = 1e-6
ROPE_THETA = 10000.0
GATE_CAP = 15.0
ML_CHUNK = 64
ML_SAMPLE_ROWS = 16
PAGES_PER_STEP = 16
ATTN_RING_SLOTS = 4
LANES = 128
SUBLANES = 8
V7X_VMEM_LIMIT = 56 * 1024 * 1024
MM_TM_PREF = 2176
SWIGLU_TM_PREF = 4352
ATTN_TQ_PREF = 512
ATTN_HEADS_PER_STEP = 4

F32 = jnp.float32
BF16 = jnp.bfloat16


def _cparams(*sem):
    return pltpu.CompilerParams(dimension_semantics=sem, vmem_limit_bytes=V7X_VMEM_LIMIT)


def _div_tile(dim, pref, align):
    if dim <= pref:
        return dim
    t = (pref // align) * align
    while t >= align:
        if dim % t == 0:
            return t
        t -= align
    return dim


def _k_tile(kdim, pref):
    exact = _div_tile(kdim, pref, 256)
    if kdim <= pref or 2 * exact > pref:
        return exact, 0
    for tk in range((pref // 256) * 256, exact, -256):
        rem = kdim % tk
        if rem and (kdim - rem) % rem == 0 and rem % LANES == 0:
            return tk, rem
    return exact, 0


def _rms(x, g):
    return x * lax.rsqrt(jnp.mean(x * x, axis=-1, keepdims=True) + EPS) * g


def _gelu(x):
    return x * (0.5 * (1.0 + jnp.tanh(0.7978845608028654 * (x + 0.044715 * (x * x * x)))))


def _sigmoid(x):
    return 0.5 * (jnp.tanh(0.5 * x) + 1.0)


def _dot(a, b):
    return jnp.dot(a, b, preferred_element_type=F32)


def _dot_nt(a, b):
    return lax.dot_general(a, b, (((1,), (1,)), ((), ())), preferred_element_type=F32)


def _dot_tn(a, b):
    return lax.dot_general(a, b, (((0,), (0,)), ((), ())), preferred_element_type=F32)


def _skewed(n, stages):
    vals = [None] * n
    for t in range(n + len(stages) - 1):
        for s, fn in enumerate(stages):
            i = t - s
            if 0 <= i < n:
                vals[i] = fn(i, vals[i])
    return vals


def _row_segments(block_start, tm, n_prompt_rows, tp):
    segs, r, end = [], block_start, block_start + tm
    while r < end:
        if r < n_prompt_rows:
            seq = r // tp
            hi = min(end, (seq + 1) * tp)
            segs.append((r - block_start, hi - block_start, "p", seq))
        else:
            hi = end
            segs.append((r - block_start, hi - block_start, "s", r - n_prompt_rows))
        r = hi
    return segs


def _mm_kernel(*refs, nk, nx, nka, w_t, has_rem, silu_x, has_bias, segments, split_out):
    it = iter(refs)
    x_refs = [next(it) for _ in range(nx)]
    w_ref = next(it)
    xr_ref = next(it) if has_rem else None
    wr_ref = next(it) if has_rem else None
    b_ref = next(it) if has_bias else None
    resp_ref = ress_ref = gp_ref = gs_ref = None
    if segments is not None:
        resp_ref, ress_ref, gp_ref, gs_ref = next(it), next(it), next(it), next(it)
    o_ref = next(it)
    os_ref = next(it) if split_out else None
    acc_ref = next(it)
    k = pl.program_id(2)

    def accum(x_ref, first):
        x = x_ref[...]
        if silu_x:
            x = x * _sigmoid(x)
        w = w_ref[...].astype(BF16)
        prod = _dot_nt(x.astype(BF16), w) if w_t else _dot(x.astype(BF16), w)
        if not first:
            acc_ref[...] += prod
        elif has_rem:
            acc_ref[...] = prod + _dot(xr_ref[...].astype(BF16), wr_ref[...].astype(BF16))
        else:
            acc_ref[...] = prod

    pl.when(k == 0)(lambda: accum(x_refs[0], True))
    if nx == 1:
        pl.when(k > 0)(lambda: accum(x_refs[0], False))
    else:
        pl.when((k > 0) & (k < nka))(lambda: accum(x_refs[0], False))
        pl.when(k >= nka)(lambda: accum(x_refs[1], False))

    @pl.when(k == nk - 1)
    def _finish():
        if segments is None:
            y = acc_ref[...]
            if has_bias:
                y = y + b_ref[...]
            o_ref[...] = y.astype(o_ref.dtype)
        else:
            i = pl.program_id(1)
            for bi, segs in enumerate(segments):
                @pl.when(i == bi)
                def _gated(segs=segs):
                    for lo, hi, kind, idx in segs:
                        if kind == "p":
                            o_ref[lo:hi, :] = resp_ref[lo:hi, :] + gp_ref[idx:idx + 1, :] * acc_ref[lo:hi, :]
                        else:
                            srows = slice(idx, idx + hi - lo)
                            y = ress_ref[srows, :] + gs_ref[srows, :] * acc_ref[lo:hi, :]
                            if split_out:
                                os_ref[srows, :] = y
                            else:
                                o_ref[lo:hi, :] = y


def _matmul(xs, w, *, layer, n_out, tm, tn, tk, out_dtype, w_t=False, k_rem=0, bias=None,
            silu_x=False, resid=None, name="mm"):
    m, k_each = xs[0].shape
    nx = len(xs)
    assert m % tm == 0 and (k_each - k_rem) % tk == 0 and (nx == 1 or k_rem == 0)
    nka = (k_each - k_rem) // tk
    nk = nka * nx
    grid = (pl.cdiv(n_out, tn), m // tm, nk)

    def spec(shape, index):
        return pl.BlockSpec(shape, lambda j, i, k: index(i, j, k))

    in_specs = [spec((tm, tk), lambda i, j, k: (i, jnp.minimum(k, nka - 1)))]
    if nx == 2:
        in_specs.append(spec((tm, tk), lambda i, j, k: (i, jnp.maximum(k - nka, 0))))
    if w_t:
        in_specs.append(spec((None, tn, tk), lambda i, j, k: (layer, j, k)))
    else:
        in_specs.append(spec((None, tk, tn), lambda i, j, k: (layer, k, j)))
    args = list(xs) + [w]
    if k_rem:
        assert not w_t
        rem_blk = (k_each - k_rem) // k_rem
        in_specs.append(spec((tm, k_rem), lambda i, j, k: (i, rem_blk)))
        in_specs.append(spec((None, k_rem, tn), lambda i, j, k: (layer, rem_blk, j)))
        args += [xs[0], w]
    if bias is not None:
        in_specs.append(spec((None, 1, tn), lambda i, j, k: (layer, 0, j)))
        args.append(bias)
    segments = None
    split_out = False
    out_specs = spec((tm, tn), lambda i, j, k: (i, j))
    out_shape = jax.ShapeDtypeStruct((m, n_out), out_dtype)
    if resid is not None:
        n_prompt_rows, tp = resid["bp"] * resid["tp"], resid["tp"]
        gate_blk = resid["gate_col"] // tn
        assert resid["gate_col"] % tn == 0 and resid["bp"] <= SUBLANES
        segments = [_row_segments(b * tm, tm, n_prompt_rows, tp) for b in range(m // tm)]
        ms = resid["gate_s"].shape[0]
        res_s_blk = resid["res_s_row0"] // ms
        assert resid["res_s_row0"] % ms == 0
        in_specs += [
            spec((tm, tn), lambda i, j, k: (i, j)),
            spec((ms, tn), lambda i, j, k: (res_s_blk, j)),
            spec((SUBLANES, tn), lambda i, j, k: (0, gate_blk + j)),
            spec((ms, tn), lambda i, j, k: (0, gate_blk + j)),
        ]
        args += [resid["res_p"], resid["res_s"], resid["mod"], resid["gate_s"]]
        split_out = resid["split_out"]
        if split_out:
            out_specs = [out_specs, spec((ms, tn), lambda i, j, k: (0, j))]
            out_shape = [jax.ShapeDtypeStruct((n_prompt_rows, n_out), out_dtype),
                         jax.ShapeDtypeStruct((ms, n_out), out_dtype)]
    kern = functools.partial(_mm_kernel, nk=nk, nx=nx, nka=nka, w_t=w_t, has_rem=bool(k_rem), silu_x=silu_x,
                             has_bias=bias is not None, segments=segments, split_out=split_out)
    return pl.pallas_call(
        kern,
        grid=grid,
        in_specs=in_specs,
        out_specs=out_specs,
        out_shape=out_shape,
        scratch_shapes=[pltpu.VMEM((tm, tn), F32)],
        compiler_params=_cparams("parallel", "arbitrary", "arbitrary"),
        name=name,
    )(*args)


SWIGLU_SUB = 256


def _swiglu_kernel(x_ref, *refs, nk, nsub):
    wg = refs[:nsub]
    wu = refs[nsub:2 * nsub]
    o_ref = refs[2 * nsub]
    accg, accu = refs[2 * nsub + 1:]
    k = pl.program_id(2)

    def accum(first):
        x = x_ref[...]
        for s in range(nsub):
            cols = slice(s * SWIGLU_SUB, (s + 1) * SWIGLU_SUB)
            pg = _dot(x, wg[s][...].astype(BF16))
            pu = _dot(x, wu[s][...].astype(BF16))
            if first:
                accg[:, cols] = pg
                accu[:, cols] = pu
            else:
                accg[:, cols] += pg
                accu[:, cols] += pu

    pl.when(k == 0)(lambda: accum(True))
    pl.when(k > 0)(lambda: accum(False))

    @pl.when(k == nk - 1)
    def _finish():
        g = accg[...]
        o_ref[...] = (g * _sigmoid(g) * accu[...]).astype(o_ref.dtype)


def _swiglu_in(x, w, *, layer, tm, tk, nsub):
    m, kdim = x.shape
    f = w.shape[2] // 2
    sub = SWIGLU_SUB
    assert f % sub == 0 and m % tm == 0 and kdim % tk == 0
    nfb = f // sub
    last = 2 * nfb - 1
    nk = kdim // tk
    tn = nsub * sub
    grid = (m // tm, pl.cdiv(f, tn), nk)
    in_specs = [pl.BlockSpec((tm, tk), lambda i, j, k: (i, k))]
    for s in range(nsub):
        in_specs.append(pl.BlockSpec((None, tk, sub), lambda i, j, k, s=s: (layer, k, j * nsub + s)))
    for s in range(nsub):
        in_specs.append(pl.BlockSpec(
            (None, tk, sub), lambda i, j, k, s=s: (layer, k, jnp.minimum(nfb + j * nsub + s, last))))
    return pl.pallas_call(
        functools.partial(_swiglu_kernel, nk=nk, nsub=nsub),
        grid=grid,
        in_specs=in_specs,
        out_specs=pl.BlockSpec((tm, tn), lambda i, j, k: (i, j)),
        out_shape=jax.ShapeDtypeStruct((m, f), BF16),
        scratch_shapes=[pltpu.VMEM((tm, tn), F32), pltpu.VMEM((tm, tn), F32)],
        compiler_params=_cparams("parallel", "parallel", "arbitrary"),
        name="swiglu_in",
    )(x, *([w] * (2 * nsub)))


def _modulate_kernel(xp_ref, xs_ref, g_ref, shp_ref, scp_ref, shs_ref, scs_ref, o_ref, *, tr, tp, bp, npb):
    i = pl.program_id(0)
    group = 16

    def sweep(x_ref, shift_of, scale_of):
        def body(r, carry):
            rows = pl.ds(pl.multiple_of(r * group, group), group)
            y = _rms(x_ref[rows, :], g_ref[...])
            o_ref[rows, :] = (y * (1.0 + scale_of(rows)) + shift_of(rows)).astype(o_ref.dtype)
            return carry
        lax.fori_loop(0, tr // group, body, 0)

    @pl.when(i < npb)
    def _prompt_rows():
        seq = jnp.minimum(i * tr // tp, bp - 1)
        shift = shp_ref[pl.ds(seq, 1), :]
        scale = scp_ref[pl.ds(seq, 1), :]
        sweep(xp_ref, lambda rows: shift, lambda rows: scale)

    @pl.when(i >= npb)
    def _sample_rows():
        sweep(xs_ref, lambda rows: shs_ref[rows, :], lambda rows: scs_ref[rows, :])


def _modulate(x_p, x_s, s_row0, g, mod, mod_s, *, layer, tr, bp, tp):
    d = x_p.shape[1]
    ms = mod_s.shape[0]
    npb = bp * tp // tr
    m = bp * tp + ms
    assert s_row0 % tr == 0
    srow = lambda i: jnp.maximum(i - npb, 0)
    return pl.pallas_call(
        functools.partial(_modulate_kernel, tr=tr, tp=tp, bp=bp, npb=npb),
        grid=(m // tr,),
        in_specs=[
            pl.BlockSpec((tr, d), lambda i: (jnp.minimum(i, npb - 1), 0)),
            pl.BlockSpec((tr, d), lambda i: (s_row0 // tr + srow(i), 0)),
            pl.BlockSpec((None, 1, d), lambda i: (layer, 0, 0)),
            pl.BlockSpec((SUBLANES, d), lambda i: (0, 0)),
            pl.BlockSpec((SUBLANES, d), lambda i: (0, 1)),
            pl.BlockSpec((tr, d), lambda i: (srow(i), 0)),
            pl.BlockSpec((tr, d), lambda i: (srow(i), 1)),
        ],
        out_specs=pl.BlockSpec((tr, d), lambda i: (i, 0)),
        out_shape=jax.ShapeDtypeStruct((m, d), BF16),
        compiler_params=_cparams("parallel"),
        name="modulate",
    )(x_p, x_s, g, mod, mod, mod_s, mod_s)


def _gmlp_kernel(u_ref, v_ref, lng_ref, lnb_ref, w_ref, b_ref, a_ref, vout_ref, *, groups, gdim):
    for g in range(groups):
        cols = slice(g * gdim, (g + 1) * gdim)
        vg = _gelu(v_ref[:, cols])
        mu = jnp.mean(vg, axis=-1, keepdims=True)
        xc = vg - mu
        var = jnp.mean(xc * xc, axis=-1, keepdims=True)
        y = xc * lax.rsqrt(var + EPS) * lng_ref[:, cols] + lnb_ref[:, cols]
        vout_ref[:, cols] = y
        mixed = _dot(w_ref[g].astype(BF16), y.astype(BF16)) + b_ref[:, g:g + 1]
        a_ref[:, cols] = (_gelu(u_ref[:, cols]) * mixed).astype(a_ref.dtype)


def _gmlp(z, ln_g, ln_b, w_mix, b_mix, *, layer, gw, n_prompt_blocks):
    m = z.shape[0]
    groups, rows = w_mix.shape[1], w_mix.shape[2]
    assert m % rows == 0
    which = lambda i: jnp.where(i < n_prompt_blocks, 0, 1)
    return pl.pallas_call(
        functools.partial(_gmlp_kernel, groups=groups, gdim=gw // groups),
        grid=(m // rows,),
        in_specs=[
            pl.BlockSpec((rows, gw), lambda i: (i, 0)),
            pl.BlockSpec((rows, gw), lambda i: (i, 1)),
            pl.BlockSpec((None, 1, gw), lambda i: (layer, 0, 0)),
            pl.BlockSpec((None, 1, gw), lambda i: (layer, 0, 0)),
            pl.BlockSpec((None, groups, rows, rows), lambda i: (which(i), 0, 0, 0)),
            pl.BlockSpec((None, rows, groups), lambda i: (which(i), 0, 0)),
        ],
        out_specs=[pl.BlockSpec((rows, gw), lambda i: (i, 0)), pl.BlockSpec((rows, gw), lambda i: (i, 0))],
        out_shape=[jax.ShapeDtypeStruct((m, gw), BF16), jax.ShapeDtypeStruct((m, gw), F32)],
        compiler_params=_cparams("parallel"),
        name="gmlp",
    )(z, z, ln_g, ln_b, w_mix, b_mix)


def _rope_rows(x, cosf, sinf):
    half = x.shape[-1] // 2
    swapped = jnp.concatenate([x[:, half:], x[:, :half]], axis=-1)
    return x * cosf + swapped * sinf


def _q_kernel(cq_ref, gqa_ref, wuq_ref, gqk_ref, cos_ref, sin_ref, wuk_ref, o_ref, cqn_sc,
              *, hb, nope, rank, scale):
    @pl.when(pl.program_id(1) == 0)
    def _norm():
        cqn_sc[...] = _rms(cq_ref[...], gqa_ref[...]).astype(BF16)

    gain = gqk_ref[...] * scale

    def project(h, _):
        return _dot(cqn_sc[...], wuq_ref[h].astype(BF16))

    def normalise(h, q):
        q = _rms(q, gain)
        o_ref[h, :, rank:] = _rope_rows(q[:, nope:], cos_ref[...], sin_ref[...]).astype(o_ref.dtype)
        return q[:, :nope].astype(BF16)

    def absorb(h, q_nope):
        o_ref[h, :, :rank] = _dot(q_nope, wuk_ref[h].astype(BF16)).astype(o_ref.dtype)

    _skewed(hb, [project, normalise, absorb])


def _mla_q(z, g_qa, w_uq_h, g_qk, cosf, sinf, w_uk, *, layer, cq_off, tm, hb, scale):
    m = z.shape[0]
    qr = g_qa.shape[-1]
    heads, _, qd = w_uq_h.shape
    nope, rank = w_uk.shape[2], w_uk.shape[3]
    rope_d = qd - nope
    assert cq_off % qr == 0 and heads % hb == 0
    cq_blk = cq_off // qr
    return pl.pallas_call(
        functools.partial(_q_kernel, hb=hb, nope=nope, rank=rank, scale=scale),
        grid=(m // tm, heads // hb),
        in_specs=[
            pl.BlockSpec((tm, qr), lambda i, h: (i, cq_blk)),
            pl.BlockSpec((None, 1, qr), lambda i, h: (layer, 0, 0)),
            pl.BlockSpec((hb, qr, qd), lambda i, h: (h, 0, 0)),
            pl.BlockSpec((None, 1, qd), lambda i, h: (layer, 0, 0)),
            pl.BlockSpec((tm, rope_d), lambda i, h: (i, 0)),
            pl.BlockSpec((tm, rope_d), lambda i, h: (i, 0)),
            pl.BlockSpec((None, hb, nope, rank), lambda i, h: (layer, h, 0, 0)),
        ],
        out_specs=pl.BlockSpec((hb, tm, rank + rope_d), lambda i, h: (h, i, 0)),
        out_shape=jax.ShapeDtypeStruct((heads, m, rank + rope_d), BF16),
        scratch_shapes=[pltpu.VMEM((tm, qr), BF16)],
        compiler_params=_cparams("parallel", "arbitrary"),
        name="mla_q",
    )(z, g_qa, w_uq_h, g_qk, cosf, sinf, w_uk)


def _kv_kernel(ckv_ref, kr_ref, gkv_ref, gkr_ref, cos_ref, sin_ref, ckv_o, kr_o, kcat_o,
               *, rank, rope_d):
    c = _rms(ckv_ref[...], gkv_ref[...])
    r = _rope_rows(_rms(kr_ref[:, :rope_d], gkr_ref[...]), cos_ref[...], sin_ref[...])
    ckv_o[...] = c
    kr_o[...] = r
    kcat_o[:, :rank] = c.astype(kcat_o.dtype)
    kcat_o[:, rank:] = r.astype(kcat_o.dtype)


def _mla_kv(z, g_kv, g_kr, cosf, sinf, *, layer, ckv_off, kr_off, tm):
    m = z.shape[0]
    rank = g_kv.shape[-1]
    rope_d = g_kr.shape[-1]
    assert ckv_off % rank == 0 and kr_off % LANES == 0
    return pl.pallas_call(
        functools.partial(_kv_kernel, rank=rank, rope_d=rope_d),
        grid=(m // tm,),
        in_specs=[
            pl.BlockSpec((tm, rank), lambda i: (i, ckv_off // rank)),
            pl.BlockSpec((tm, LANES), lambda i: (i, kr_off // LANES)),
            pl.BlockSpec((None, 1, rank), lambda i: (layer, 0, 0)),
            pl.BlockSpec((None, 1, rope_d), lambda i: (layer, 0, 0)),
            pl.BlockSpec((tm, rope_d), lambda i: (i, 0)),
            pl.BlockSpec((tm, rope_d), lambda i: (i, 0)),
        ],
        out_specs=[
            pl.BlockSpec((tm, rank), lambda i: (i, 0)),
            pl.BlockSpec((tm, rope_d), lambda i: (i, 0)),
            pl.BlockSpec((tm, rank + rope_d), lambda i: (i, 0)),
        ],
        out_shape=[
            jax.ShapeDtypeStruct((m, rank), F32),
            jax.ShapeDtypeStruct((m, rope_d), F32),
            jax.ShapeDtypeStruct((m, rank + rope_d), BF16),
        ],
        compiler_params=_cparams("parallel"),
        name="mla_kv",
    )(z, z, g_kv, g_kr, cosf, sinf)


def _pattn_kernel(q_ref, k_ref, wuv_ref, prev_ref, o_ref, m_sc, l_sc, acc_sc, *, heads, tq, rank, vd):
    del prev_ref
    qi = pl.program_id(2)

    m_sc[...] = jnp.full_like(m_sc, -jnp.inf)
    l_sc[...] = jnp.zeros_like(l_sc)
    acc_sc[...] = jnp.zeros_like(acc_sc)

    def step(kb, masked):
        kblk = k_ref[pl.ds(pl.multiple_of(kb * tq, tq), tq), :]
        vblk = kblk[:, :rank]
        if masked:
            visible = (lax.broadcasted_iota(jnp.int32, (tq, tq), 1)
                       <= lax.broadcasted_iota(jnp.int32, (tq, tq), 0))
        def scores(h, _):
            s = _dot_nt(q_ref[h], kblk)
            return jnp.where(visible, s, -jnp.inf) if masked else s

        def softmax(h, s):
            rs = slice(h * tq, (h + 1) * tq)
            m_prev = m_sc[rs, :]
            m_new = jnp.maximum(m_prev, jnp.max(s, axis=-1, keepdims=True))
            alpha = jnp.exp(m_prev - m_new)
            p = jnp.exp(s - m_new)
            l_sc[rs, :] = alpha * l_sc[rs, :] + jnp.sum(p, axis=-1, keepdims=True)
            m_sc[rs, :] = m_new
            return alpha, p.astype(BF16)

        def values(h, ap):
            rs = slice(h * tq, (h + 1) * tq)
            acc_sc[rs, :] = ap[0] * acc_sc[rs, :] + _dot(ap[1], vblk)

        _skewed(heads, [scores, softmax, values])

    def body(kb, carry):
        step(kb, False)
        return carry

    lax.fori_loop(0, qi, body, 0)
    step(qi, True)
    for h in range(heads):
        rs = slice(h * tq, (h + 1) * tq)
        o_lat = (acc_sc[rs, :] / l_sc[rs, :]).astype(BF16)
        o_ref[:, h * vd:(h + 1) * vd] = _dot(o_lat, wuv_ref[h].astype(BF16)).astype(o_ref.dtype)


def _prompt_attention(qcat, kcat, w_uv, *, layer, bp, tp, tq):
    heads, m, qd = qcat.shape
    rank, vd = w_uv.shape[2], w_uv.shape[3]
    nq = tp // tq
    hg = _div_tile(heads, ATTN_HEADS_PER_STEP, 1)
    return pl.pallas_call(
        functools.partial(_pattn_kernel, heads=hg, tq=tq, rank=rank, vd=vd),
        grid=(bp, heads // hg, nq),
        in_specs=[
            pl.BlockSpec((hg, tq, qd), lambda b, g, qi: (g, b * nq + qi, 0)),
            pl.BlockSpec((tp, qd), lambda b, g, qi: (b, 0)),
            pl.BlockSpec((None, hg, rank, vd), lambda b, g, qi: (layer, g, 0, 0)),
            pl.BlockSpec(memory_space=pl.ANY),
        ],
        out_specs=pl.BlockSpec((tq, hg * vd), lambda b, g, qi: (b * nq + qi, g)),
        out_shape=jax.ShapeDtypeStruct((m, heads * vd), BF16),
        input_output_aliases={3: 0},
        scratch_shapes=[pltpu.VMEM((hg * tq, 1), F32), pltpu.VMEM((hg * tq, 1), F32),
                        pltpu.VMEM((hg * tq, rank), F32)],
        compiler_params=_cparams("parallel", "parallel", "arbitrary"),
        name="prompt_attn",
    )(qcat, kcat, w_uv, jnp.zeros((m, heads * vd), BF16))


def _sattn_kernel(pt_ref, q_ref, knew_ref, ckv_hbm, krt_hbm, o_ref, ckv_ring, krt_ring, kbuf, krt_buf, sems,
                  m_sc, l_sc, acc_sc, *, layer, pb, page, rank, nchunks, total, t_new):
    b = pl.program_id(0)
    slots = ATTN_RING_SLOTS
    ahead = slots - 1

    def chunk_copies(page_ids, slot):
        copies = []
        for i, pid in enumerate(page_ids):
            copies.append(pltpu.make_async_copy(
                ckv_hbm.at[layer, pid], ckv_ring.at[slot, pl.ds(i * page, page), :], sems.at[slot]))
            copies.append(pltpu.make_async_copy(krt_hbm.at[layer, pid], krt_ring.at[slot, i], sems.at[slot]))
        return copies

    def start_chunk(g):
        src = lax.rem(g, total)
        for cp in chunk_copies([pt_ref[src * pb + i] for i in range(pb)], lax.rem(g, slots)):
            cp.start()

    def wait_chunk(g):
        for cp in chunk_copies([0] * pb, lax.rem(g, slots)):
            cp.wait()

    @pl.when(b == 0)
    def _prime():
        for g0 in range(ahead):
            start_chunk(g0)

    m_sc[...] = jnp.full_like(m_sc, -jnp.inf)
    l_sc[...] = jnp.zeros_like(l_sc)
    acc_sc[...] = jnp.zeros_like(acc_sc)
    q = q_ref[...]

    def online_update(s, values):
        m_prev = m_sc[...]
        m_new = jnp.maximum(m_prev, jnp.max(s, axis=-1, keepdims=True))
        alpha = jnp.exp(m_prev - m_new)
        p = jnp.exp(s - m_new)
        l_sc[...] = alpha * l_sc[...] + jnp.sum(p, axis=-1, keepdims=True)
        acc_sc[...] = alpha * acc_sc[...] + _dot(p.astype(BF16), values)
        m_sc[...] = m_new

    def chunk(c, carry):
        g = b * nchunks + c
        slot = lax.rem(g, slots)
        wait_chunk(g)
        start_chunk(g + ahead)
        kbuf[...] = ckv_ring[slot].astype(BF16)
        for i in range(pb):
            krt_buf[:, i * page:(i + 1) * page] = krt_ring[slot, i].astype(BF16)
        s = _dot_nt(q[:, :rank], kbuf[...]) + _dot(q[:, rank:], krt_buf[...])
        online_update(s, kbuf[...])
        return carry

    lax.fori_loop(0, nchunks, chunk, 0)

    k_new = knew_ref[...]
    s_new = _dot_nt(q, k_new)
    row = lax.rem(lax.broadcasted_iota(jnp.int32, s_new.shape, 0), t_new)
    col = lax.broadcasted_iota(jnp.int32, s_new.shape, 1)
    online_update(jnp.where(col <= row, s_new, -jnp.inf), k_new[:, :rank])
    o_ref[...] = (acc_sc[...] / l_sc[...]).astype(o_ref.dtype)

    @pl.when(b == pl.num_programs(0) - 1)
    def _drain():
        for g1 in range(ahead):
            wait_chunk(total + g1)


def _sample_attention(q_s, k_new, cache_ckv, cache_krope_t, page_table, *, layer, t_new):
    b, rows, qd = q_s.shape
    page, rank = cache_ckv.shape[2], cache_ckv.shape[3]
    rope_d = cache_krope_t.shape[2]
    n_pages = page_table.shape[1]
    pb = _div_tile(n_pages, PAGES_PER_STEP, 1)
    nchunks = n_pages // pb
    total = b * nchunks
    assert total >= ATTN_RING_SLOTS
    grid_spec = pltpu.PrefetchScalarGridSpec(
        num_scalar_prefetch=1,
        grid=(b,),
        in_specs=[
            pl.BlockSpec((None, rows, qd), lambda bi, pt: (bi, 0, 0)),
            pl.BlockSpec((None, k_new.shape[1], qd), lambda bi, pt: (bi, 0, 0)),
            pl.BlockSpec(memory_space=pl.ANY),
            pl.BlockSpec(memory_space=pl.ANY),
        ],
        out_specs=pl.BlockSpec((None, rows, rank), lambda bi, pt: (bi, 0, 0)),
        scratch_shapes=[pltpu.VMEM((ATTN_RING_SLOTS, pb * page, rank), cache_ckv.dtype),
                        pltpu.VMEM((ATTN_RING_SLOTS, pb, rope_d, page), cache_krope_t.dtype),
                        pltpu.VMEM((pb * page, rank), BF16), pltpu.VMEM((rope_d, pb * page), BF16),
                        pltpu.SemaphoreType.DMA((ATTN_RING_SLOTS,)),
                        pltpu.VMEM((rows, 1), F32), pltpu.VMEM((rows, 1), F32),
                        pltpu.VMEM((rows, rank), F32)],
    )
    return pl.pallas_call(
        functools.partial(_sattn_kernel, layer=layer, pb=pb, page=page, rank=rank, nchunks=nchunks,
                          total=total, t_new=t_new),
        grid_spec=grid_spec,
        out_shape=jax.ShapeDtypeStruct((b, rows, rank), BF16),
        compiler_params=_cparams("arbitrary"),
        name="sample_attn",
    )(page_table.reshape(-1), q_s, k_new, cache_ckv, cache_krope_t)


def _uv_kernel(x_ref, w_ref, prev_ref, o_ref):
    del prev_ref
    o_ref[...] = _dot(x_ref[...], w_ref[...].astype(BF16)).astype(o_ref.dtype)


def _value_up(o_lat, w_uv, o_prev, *, layer, row0):
    heads, ms, rank = o_lat.shape
    vd = w_uv.shape[3]
    assert row0 % ms == 0
    return pl.pallas_call(
        _uv_kernel,
        grid=(heads,),
        in_specs=[
            pl.BlockSpec((None, ms, rank), lambda h: (h, 0, 0)),
            pl.BlockSpec((None, None, rank, vd), lambda h: (layer, h, 0, 0)),
            pl.BlockSpec(memory_space=pl.ANY),
        ],
        out_specs=pl.BlockSpec((ms, vd), lambda h: (row0 // ms, h)),
        out_shape=jax.ShapeDtypeStruct(o_prev.shape, o_prev.dtype),
        input_output_aliases={2: 0},
        compiler_params=_cparams("parallel"),
        name="value_up",
    )(o_lat, w_uv, o_prev)


def _mlstm_kernel(*refs, heads, dk, dv, rows, sub_rows, nsub, aliased):
    (q_ref, k_ref, v_ref, o_ref, gc_ref, gr_ref, bgc_ref, bgr_ref, gmh_ref, c0_ref, n0_ref, m0_ref) = refs[:12]
    h_ref, c_ref, n_ref, m_ref = refs[-4:]
    sub = pl.program_id(2)

    @pl.when(pl.program_id(1) == 0)
    def _load_state():
        c_ref[...] = c0_ref[...]
        n_ref[...] = n0_ref[...]
        m_ref[...] = m0_ref[...]

    if nsub > 1:
        @pl.when(sub == 0)
        def _clear_rows():
            h_ref[...] = jnp.zeros_like(h_ref)

    n_all = n_ref[...]
    m_all = m_ref[...]
    n_out = n_all
    m_out = m_all
    head_id = lax.broadcasted_iota(jnp.int32, (heads, 1), 0)

    g_col = GATE_CAP * jnp.tanh((gc_ref[:, :2 * heads] + bgc_ref[...]) / GATE_CAP)
    g_row = GATE_CAP * jnp.tanh((gr_ref[...] + bgr_ref[...]) / GATE_CAP)

    def log_sigmoid(x):
        return jnp.minimum(x, 0.0) - jnp.log(1.0 + jnp.exp(-jnp.abs(x)))

    li = lax.broadcasted_iota(jnp.int32, (rows, rows), 0)
    si = lax.broadcasted_iota(jnp.int32, (rows, rows), 1)
    causal = si <= li
    ri = lax.broadcasted_iota(jnp.int32, (rows, 1), 0)
    ci = lax.broadcasted_iota(jnp.int32, (1, rows), 1)
    lo, hi = sub * sub_rows, (sub + 1) * sub_rows
    row_ok = (ri >= lo) & (ri < hi)
    col_ok = (ci >= lo) & (ci < hi)
    k_scale = dk ** -0.5

    state_rows = []

    def gates(h, _):
        ig_c = jnp.where(row_ok, g_col[:, h:h + 1], -jnp.inf)
        lf_c = jnp.where(row_ok, log_sigmoid(g_col[:, heads + h:heads + h + 1]), 0.0)
        ig_r = jnp.where(col_ok, g_row[h:h + 1, :], -jnp.inf)
        lf_r = jnp.where(col_ok, log_sigmoid(g_row[heads + h:heads + h + 1, :]), 0.0)
        b_c = jnp.sum(jnp.where(causal, lf_r, 0.0), axis=1, keepdims=True)
        b_r = jnp.sum(jnp.where(li <= si, lf_c, 0.0), axis=0, keepdims=True)
        b_last = jnp.sum(lf_r, axis=1, keepdims=True)
        m0 = m_all[h:h + 1, :]
        dmat = jnp.where(causal, b_c - b_r + ig_r, -jnp.inf)
        inter = b_c + m0
        m_pos = jnp.maximum(inter, jnp.max(dmat, axis=1, keepdims=True))
        g_r = b_last - b_r + ig_r
        m_new = jnp.maximum(b_last + m0, jnp.max(g_r, axis=1, keepdims=True))
        return dict(decay_w=jnp.exp(dmat - m_pos), a=jnp.exp(inter - m_pos), floor=jnp.exp(-m_pos),
                    m_new=m_new, decay=jnp.exp(b_last + m0 - m_new), wk_c=jnp.exp(b_last - b_c + ig_c - m_new))

    def products(h, st):
        qb = q_ref[:, h * dk:(h + 1) * dk].astype(BF16)
        kh = k_ref[:, h * dk:(h + 1) * dk] * k_scale
        st["qk"] = _dot_nt(qb, kh.astype(BF16))
        st["qc"] = _dot(qb, c_ref[h].astype(BF16))
        return st

    def update(h, st):
        qh = q_ref[:, h * dk:(h + 1) * dk]
        kh = k_ref[:, h * dk:(h + 1) * dk] * k_scale
        vb = v_ref[:, h * dv:(h + 1) * dv].astype(BF16)
        n0 = n_all[h:h + 1, :]
        w = st["decay_w"] * st["qk"]
        st["num"] = st["a"] * st["qc"] + _dot(w.astype(BF16), vb)
        st["nq"] = st["a"] * jnp.sum(qh * n0, axis=1, keepdims=True) + jnp.sum(w, axis=1, keepdims=True)
        kw = st["wk_c"] * kh
        c_ref[h] = st["decay"] * c_ref[h] + _dot_tn(kw.astype(BF16), vb)
        state_rows.append((st["decay"] * n0 + jnp.sum(kw, axis=0, keepdims=True), st["m_new"]))
        return st

    def emit(h, st):
        hs = st["num"] / jnp.maximum(jnp.abs(st["nq"]), st["floor"])
        hn = _rms(hs, gmh_ref[:, h * dv:(h + 1) * dv])
        out = (_sigmoid(o_ref[:, h * dv:(h + 1) * dv]) * hn).astype(h_ref.dtype)
        cols = slice(h * dv, (h + 1) * dv)
        if nsub == 1:
            h_ref[:, cols] = out
        else:
            h_ref[:, cols] = jnp.where(row_ok, out, h_ref[:, cols])

    _skewed(heads, [gates, products, update, emit])
    for h, (n_new, m_new) in enumerate(state_rows):
        n_out = jnp.where(head_id == h, n_new, n_out)
        m_out = jnp.where(head_id == h, m_new, m_out)
    n_ref[...] = n_out
    m_ref[...] = m_out


def _mlstm(z, g_rows, b_gates_c, b_gates_r, g_mh, c0, n0, m0, h_prev, *, layer, nb, nc, nsub,
           rows, sub_rows, row_blk0):
    m = z.shape[0]
    _, heads, dk, dv = c0.shape
    qw, vw = heads * dk, heads * dv
    assert (2 * qw) % vw == 0 and (2 * qw + 2 * vw) % LANES == 0
    g_blk = (2 * qw + 2 * vw) // LANES
    blk = lambda b, c, s: row_blk0 + b * nc + c
    seq = lambda b, c, s: b * nsub + s
    in_specs = [
        pl.BlockSpec((rows, qw), lambda b, c, s: (blk(b, c, s), 0)),
        pl.BlockSpec((rows, qw), lambda b, c, s: (blk(b, c, s), 1)),
        pl.BlockSpec((rows, vw), lambda b, c, s: (blk(b, c, s), 2 * qw // vw)),
        pl.BlockSpec((rows, vw), lambda b, c, s: (blk(b, c, s), 2 * qw // vw + 1)),
        pl.BlockSpec((rows, LANES), lambda b, c, s: (blk(b, c, s), g_blk)),
        pl.BlockSpec((None, 2 * heads, rows), lambda b, c, s: (b * nc + c, 0, 0)),
        pl.BlockSpec((None, 1, 2 * heads), lambda b, c, s: (layer, 0, 0)),
        pl.BlockSpec((None, 2 * heads, 1), lambda b, c, s: (layer, 0, 0)),
        pl.BlockSpec((None, 1, vw), lambda b, c, s: (layer, 0, 0)),
        pl.BlockSpec((None, heads, dk, dv), lambda b, c, s: (seq(b, c, s), 0, 0, 0)),
        pl.BlockSpec((None, heads, dk), lambda b, c, s: (seq(b, c, s), 0, 0)),
        pl.BlockSpec((None, heads, 1), lambda b, c, s: (seq(b, c, s), 0, 0)),
    ]
    args = [z, z, z, z, z, g_rows, b_gates_c, b_gates_r, g_mh, c0, n0, m0]
    aliases = {}
    if h_prev is not None:
        in_specs.append(pl.BlockSpec(memory_space=pl.ANY))
        args.append(h_prev)
        aliases = {12: 0}
    return pl.pallas_call(
        functools.partial(_mlstm_kernel, heads=heads, dk=dk, dv=dv, rows=rows, sub_rows=sub_rows,
                          nsub=nsub, aliased=h_prev is not None),
        grid=(nb, nc, nsub),
        in_specs=in_specs,
        out_specs=[
            pl.BlockSpec((rows, vw), lambda b, c, s: (blk(b, c, s), 0)),
            pl.BlockSpec((None, heads, dk, dv), lambda b, c, s: (seq(b, c, s), 0, 0, 0)),
            pl.BlockSpec((None, heads, dk), lambda b, c, s: (seq(b, c, s), 0, 0)),
            pl.BlockSpec((None, heads, 1), lambda b, c, s: (seq(b, c, s), 0, 0)),
        ],
        out_shape=[
            jax.ShapeDtypeStruct((m, vw), BF16),
            jax.ShapeDtypeStruct((nb * nsub, heads, dk, dv), F32),
            jax.ShapeDtypeStruct((nb * nsub, heads, dk), F32),
            jax.ShapeDtypeStruct((nb * nsub, heads, 1), F32),
        ],
        input_output_aliases=aliases,
        compiler_params=_cparams("parallel", "arbitrary", "arbitrary"),
        name="mlstm",
    )(*args)


def _rope_tables(pos, rope_d):
    half = rope_d // 2
    inv = ROPE_THETA ** (-jnp.arange(half, dtype=F32) * (2.0 / rope_d))
    ang = pos.astype(F32)[:, None] * inv[None, :]
    cos, sin = jnp.cos(ang), jnp.sin(ang)
    return jnp.concatenate([cos, cos], axis=-1), jnp.concatenate([-sin, sin], axis=-1)


def kernel(x_prompt, x_sample, c_prompt, c_sample, cache_ckv, cache_krope, page_table,
           state_mlstm_C, state_mlstm_n, state_mlstm_m,
           norm_mix_g, ada_mix_w, ada_mix_b, norm_ffn_g, ada_ffn_w, ada_ffn_b, ffn_w_in, ffn_w_out,
           ev_w_in, ev_ln_v_g, ev_ln_v_b, ev_w_s, ev_b_s, ev_g_qa, ev_w_uq, ev_g_qk, ev_g_kv, ev_g_kr,
           ev_w_uk, ev_w_uv, ev_w_out, od_w_in, od_b_gates, od_g_mh, od_w_out):
    bp, tp, d = x_prompt.shape
    bs, ts, _ = x_sample.shape
    depth = norm_mix_g.shape[0]
    mp, ms = bp * tp, bs * ts
    m = mp + ms
    past = page_table.shape[1] * cache_ckv.shape[2]

    n_cond = bp + bs
    cond_rows = -(-n_cond // 16) * 16
    c_all = jnp.concatenate([c_prompt, c_sample, jnp.zeros((cond_rows - n_cond, d), F32)], axis=0)

    def ada(w, b, l):
        mod = _matmul([c_all], w, layer=l, n_out=3 * d, tm=cond_rows, tn=_div_tile(3 * d, 1024, LANES),
                      tk=_div_tile(d, 2048, LANES), out_dtype=F32, bias=b.reshape(depth, 1, 3 * d),
                      silu_x=True, name="ada")
        return mod, jnp.repeat(mod[bp:n_cond], ts, axis=0)

    tm = _div_tile(m, MM_TM_PREF, 16)
    tr = _div_tile(ms, 256, 16)
    assert tp % tr == 0

    def residual(xs_in, w, l, x, mod, mod_s, name, split_out=False):
        tk, k_rem = _k_tile(xs_in[0].shape[1], 1536)
        resid = dict(res_p=x[0], res_s=x[1], res_s_row0=x[2], mod=mod, gate_s=mod_s, gate_col=2 * d,
                     bp=bp, tp=tp, split_out=split_out)
        out = _matmul(xs_in, w, layer=l, n_out=d, tm=tm, tn=_div_tile(d, 512, LANES), tk=tk, k_rem=k_rem,
                      out_dtype=F32, resid=resid, name=name)
        return (out[0], out[1], 0) if split_out else (out, out, mp)

    gw = ev_ln_v_g.shape[-1]
    groups, gm_chunk = ev_w_s.shape[1], ev_w_s.shape[2]
    qr_rank = ev_g_qa.shape[-1]
    heads = ev_w_uk.shape[1]
    nope, rank = ev_w_uk.shape[2], ev_w_uk.shape[3]
    rope_d = ev_g_kr.shape[-1]
    qd = rank + rope_d
    ev_in = ev_w_in.shape[-1]
    cq_off, ckv_off, kr_off = 2 * gw, 2 * gw + qr_rank, 2 * gw + qr_rank + rank
    mla_scale = float(nope + rope_d) ** -0.5
    pos = jnp.concatenate([jnp.tile(jnp.arange(tp), bp), jnp.tile(past + jnp.arange(ts), bs)])
    cosf, sinf = _rope_tables(pos, rope_d)
    ev_w_in_t = jnp.swapaxes(ev_w_in, 1, 2)
    od_w_in_t = jnp.swapaxes(od_w_in, 1, 2)
    cache_krope_t = jnp.swapaxes(cache_krope, 2, 3)

    def mix_matrix(w_s, b_s, t):
        lc = min(t, gm_chunk)
        reps = gm_chunk // lc
        w = jnp.where(jnp.tril(jnp.ones((lc, lc), dtype=bool)), w_s[:, :lc, :lc], 0.0)
        w_blk = jnp.einsum("ab,gts->gatbs", jnp.eye(reps, dtype=F32), w).reshape(groups, gm_chunk, gm_chunk)
        b_blk = jnp.tile(b_s[:, :lc], (1, reps)).T
        return w_blk, b_blk

    ml_heads, ml_dk, ml_dv = state_mlstm_C.shape[2], state_mlstm_C.shape[3], state_mlstm_C.shape[4]
    lp = min(tp, ML_CHUNK)
    ncp = tp // lp
    ml_sub = ML_SAMPLE_ROWS // ts
    assert ML_SAMPLE_ROWS % ts == 0 and bs % ml_sub == 0 and mp % ML_SAMPLE_ROWS == 0

    x = (x_prompt.reshape(mp, d), x_sample.reshape(ms, d), 0)
    ckv_p, kr_p, ckv_s, kr_s, v_s = [], [], [], [], []
    cp_l, np_l, mp_l, cs_l, ns_l, ms_l = [], [], [], [], [], []
    for l in range(depth):
        j = l // 2
        mod, mod_s = ada(ada_mix_w, ada_mix_b, l)
        h = _modulate(*x, norm_mix_g.reshape(depth, 1, d), mod, mod_s, layer=l, tr=tr, bp=bp, tp=tp)
        if l % 2 == 0:
            z = _matmul([h], ev_w_in_t, layer=j, n_out=ev_in, tm=tm, tn=min(ev_in, 1024),
                        tk=_div_tile(d, 1024, LANES), w_t=True, out_dtype=F32, name="ev_in")
            w_p, b_p = mix_matrix(ev_w_s[j], ev_b_s[j], tp)
            w_smp, b_smp = mix_matrix(ev_w_s[j], ev_b_s[j], ts)
            a, v_ln = _gmlp(z, ev_ln_v_g.reshape(-1, 1, gw), ev_ln_v_b.reshape(-1, 1, gw),
                            jnp.stack([w_p, w_smp]), jnp.stack([b_p, b_smp]),
                            layer=j, gw=gw, n_prompt_blocks=mp // gm_chunk)
            tmq = _div_tile(ms, 512, 16)
            w_uq_h = ev_w_uq[j].reshape(qr_rank, heads, nope + rope_d).transpose(1, 0, 2)
            qcat = _mla_q(z, ev_g_qa.reshape(-1, 1, qr_rank), w_uq_h, ev_g_qk.reshape(-1, 1, nope + rope_d),
                          cosf, sinf, ev_w_uk, layer=j, cq_off=cq_off, tm=tmq, hb=_div_tile(heads, 4, 1),
                          scale=mla_scale)
            ckv_n, kr_n, kcat = _mla_kv(z, ev_g_kv.reshape(-1, 1, rank), ev_g_kr.reshape(-1, 1, rope_d),
                                        cosf, sinf, layer=j, ckv_off=ckv_off, kr_off=kr_off, tm=tmq)
            o = _prompt_attention(qcat, kcat, ev_w_uv, layer=j, bp=bp, tp=tp, tq=_div_tile(tp, ATTN_TQ_PREF, 16))
            q_s = qcat[:, mp:].reshape(heads, bs, ts, qd).transpose(1, 0, 2, 3).reshape(bs, heads * ts, qd)
            k_new = jnp.pad(kcat[mp:].reshape(bs, ts, qd), ((0, 0), (0, LANES - ts), (0, 0)))
            o_lat_s = _sample_attention(q_s, k_new, cache_ckv, cache_krope_t, page_table, layer=j, t_new=ts)
            o_lat_s = o_lat_s.reshape(bs, heads, ts, rank).transpose(1, 0, 2, 3).reshape(heads, ms, rank)
            o = _value_up(o_lat_s, ev_w_uv, o, layer=j, row0=mp)
            x = residual([a, o], ev_w_out, j, x, mod, mod_s, "ev_out")
            ckv_p.append(ckv_n[:mp].reshape(bp, tp, rank))
            kr_p.append(kr_n[:mp].reshape(bp, tp, rope_d))
            ckv_s.append(ckv_n[mp:].reshape(bs, ts, rank))
            kr_s.append(kr_n[mp:].reshape(bs, ts, rope_d))
            v_s.append(v_ln[mp:].reshape(bs, ts, gw))
        else:
            od_in = od_w_in.shape[-1]
            gate_off = od_in - 2 * ml_heads
            bg_c = od_b_gates.reshape(-1, 1, 2 * ml_heads)
            bg_r = od_b_gates.reshape(-1, 2 * ml_heads, 1)
            gmh = od_g_mh.reshape(-1, 1, ml_heads * ml_dv)
            z = _matmul([h], od_w_in_t, layer=j, n_out=od_in, tm=tm, tn=min(od_in, 1024),
                        tk=_div_tile(d, 1024, LANES), w_t=True, out_dtype=F32, name="od_in")
            gates = z[:, gate_off:]
            g_rows_p = jnp.swapaxes(gates[:mp].reshape(bp * ncp, lp, 2 * ml_heads), 1, 2)
            g_rows_s = jnp.swapaxes(gates[mp:].reshape(ms // ML_SAMPLE_ROWS, ML_SAMPLE_ROWS, 2 * ml_heads), 1, 2)
            zeros = lambda *shape: jnp.zeros(shape, F32)
            hm, c_p, n_p, m_p = _mlstm(z, g_rows_p, bg_c, bg_r, gmh, zeros(bp, ml_heads, ml_dk, ml_dv),
                                       zeros(bp, ml_heads, ml_dk), zeros(bp, ml_heads, 1),
                                       jnp.zeros((m, ml_heads * ml_dv), BF16),
                                       layer=j, nb=bp, nc=ncp, nsub=1, rows=lp, sub_rows=lp, row_blk0=0)
            hm, c_s, n_s, m_s = _mlstm(z, g_rows_s, bg_c, bg_r, gmh, state_mlstm_C[j], state_mlstm_n[j],
                                       state_mlstm_m[j].reshape(bs, ml_heads, 1), hm,
                                       layer=j, nb=bs // ml_sub, nc=1, nsub=ml_sub, rows=ML_SAMPLE_ROWS,
                                       sub_rows=ts, row_blk0=mp // ML_SAMPLE_ROWS)
            x = residual([hm], od_w_out, j, x, mod, mod_s, "od_out")
            cp_l.append(c_p)
            np_l.append(n_p)
            mp_l.append(m_p.reshape(bp, ml_heads))
            cs_l.append(c_s)
            ns_l.append(n_s)
            ms_l.append(m_s.reshape(bs, ml_heads))

        mod, mod_s = ada(ada_ffn_w, ada_ffn_b, l)
        h = _modulate(*x, norm_ffn_g.reshape(depth, 1, d), mod, mod_s, layer=l, tr=tr, bp=bp, tp=tp)
        act = _swiglu_in(h, ffn_w_in, layer=l, tm=_div_tile(m, SWIGLU_TM_PREF, 16),
                         tk=_div_tile(d, 512, LANES), nsub=2)
        x = residual([act], ffn_w_out, l, x, mod, mod_s, "ffn_out", split_out=l == depth - 1)

    return (x[0].reshape(bp, tp, d), x[1].reshape(bs, ts, d),
            jnp.stack(ckv_p), jnp.stack(kr_p), jnp.stack(ckv_s), jnp.stack(kr_s), jnp.stack(v_s),
            jnp.stack(cp_l), jnp.stack(np_l), jnp.stack(mp_l),
            jnp.stack(cs_l), jnp.stack(ns_l), jnp.stack(ms_l))
```

```python
import functools

import jax
import jax.numpy as jnp
from jax import lax
from jax.experimental import pallas as pl
from jax.experimental.pallas import tpu as pltpu

EPS = 1e-6
ROPE_THETA = 10000.0
GATE_CAP = 15.0
ML_CHUNK = 64
ML_SAMPLE_ROWS = 16
PAGES_PER_STEP = 16
ATTN_RING_SLOTS = 4
LANES = 128
SUBLANES = 8
V7X_VMEM_LIMIT = 56 * 1024 * 1024
MM_TM_PREF = 2176
SWIGLU_TM_PREF = 4352
ATTN_TQ_PREF = 512
ATTN_HEADS_PER_STEP = 4

F32 = jnp.float32
BF16 = jnp.bfloat16


def _cparams(*sem):
    return pltpu.CompilerParams(dimension_semantics=sem, vmem_limit_bytes=V7X_VMEM_LIMIT)


def _div_tile(dim, pref, align):
    if dim <= pref:
        return dim
    t = (pref // align) * align
    while t >= align:
        if dim % t == 0:
            return t
        t -= align
    return dim


def _k_tile(kdim, pref):
    exact = _div_tile(kdim, pref, 256)
    if kdim <= pref or 2 * exact > pref:
        return exact, 0
    for tk in range((pref // 256) * 256, exact, -256):
        rem = kdim % tk
        if rem and (kdim - rem) % rem == 0 and rem % LANES == 0:
            return tk, rem
    return exact, 0


def _rms(x, g):
    return x * lax.rsqrt(jnp.mean(x * x, axis=-1, keepdims=True) + EPS) * g


def _gelu(x):
    return x * (0.5 * (1.0 + jnp.tanh(0.7978845608028654 * (x + 0.044715 * (x * x * x)))))


def _sigmoid(x):
    return 0.5 * (jnp.tanh(0.5 * x) + 1.0)


def _dot(a, b):
    return jnp.dot(a, b, preferred_element_type=F32)


def _dot_nt(a, b):
    return lax.dot_general(a, b, (((1,), (1,)), ((), ())), preferred_element_type=F32)


def _dot_tn(a, b):
    return lax.dot_general(a, b, (((0,), (0,)), ((), ())), preferred_element_type=F32)


def _skewed(n, stages):
    vals = [None] * n
    for t in range(n + len(stages) - 1):
        for s, fn in enumerate(stages):
            i = t - s
            if 0 <= i < n:
                vals[i] = fn(i, vals[i])
    return vals


def _row_segments(block_start, tm, n_prompt_rows, tp):
    segs, r, end = [], block_start, block_start + tm
    while r < end:
        if r < n_prompt_rows:
            seq = r // tp
            hi = min(end, (seq + 1) * tp)
            segs.append((r - block_start, hi - block_start, "p", seq))
        else:
            hi = end
            segs.append((r - block_start, hi - block_start, "s", r - n_prompt_rows))
        r = hi
    return segs


def _mm_kernel(*refs, nk, nx, nka, w_t, has_rem, silu_x, has_bias, segments, split_out):
    it = iter(refs)
    x_refs = [next(it) for _ in range(nx)]
    w_ref = next(it)
    xr_ref = next(it) if has_rem else None
    wr_ref = next(it) if has_rem else None
    b_ref = next(it) if has_bias else None
    resp_ref = ress_ref = gp_ref = gs_ref = None
    if segments is not None:
        resp_ref, ress_ref, gp_ref, gs_ref = next(it), next(it), next(it), next(it)
    o_ref = next(it)
    os_ref = next(it) if split_out else None
    acc_ref = next(it)
    k = pl.program_id(2)

    def accum(x_ref, first):
        x = x_ref[...]
        if silu_x:
            x = x * _sigmoid(x)
        w = w_ref[...].astype(BF16)
        prod = _dot_nt(x.astype(BF16), w) if w_t else _dot(x.astype(BF16), w)
        if not first:
            acc_ref[...] += prod
        elif has_rem:
            acc_ref[...] = prod + _dot(xr_ref[...].astype(BF16), wr_ref[...].astype(BF16))
        else:
            acc_ref[...] = prod

    pl.when(k == 0)(lambda: accum(x_refs[0], True))
    if nx == 1:
        pl.when(k > 0)(lambda: accum(x_refs[0], False))
    else:
        pl.when((k > 0) & (k < nka))(lambda: accum(x_refs[0], False))
        pl.when(k >= nka)(lambda: accum(x_refs[1], False))

    @pl.when(k == nk - 1)
    def _finish():
        if segments is None:
            y = acc_ref[...]
            if has_bias:
                y = y + b_ref[...]
            o_ref[...] = y.astype(o_ref.dtype)
        else:
            i = pl.program_id(1)
            for bi, segs in enumerate(segments):
                @pl.when(i == bi)
                def _gated(segs=segs):
                    for lo, hi, kind, idx in segs:
                        if kind == "p":
                            o_ref[lo:hi, :] = resp_ref[lo:hi, :] + gp_ref[idx:idx + 1, :] * acc_ref[lo:hi, :]
                        else:
                            srows = slice(idx, idx + hi - lo)
                            y = ress_ref[srows, :] + gs_ref[srows, :] * acc_ref[lo:hi, :]
                            if split_out:
                                os_ref[srows, :] = y
                            else:
                                o_ref[lo:hi, :] = y


def _matmul(xs, w, *, layer, n_out, tm, tn, tk, out_dtype, w_t=False, k_rem=0, bias=None,
            silu_x=False, resid=None, name="mm"):
    m, k_each = xs[0].shape
    nx = len(xs)
    assert m % tm == 0 and (k_each - k_rem) % tk == 0 and (nx == 1 or k_rem == 0)
    nka = (k_each - k_rem) // tk
    nk = nka * nx
    grid = (pl.cdiv(n_out, tn), m // tm, nk)

    def spec(shape, index):
        return pl.BlockSpec(shape, lambda j, i, k: index(i, j, k))

    in_specs = [spec((tm, tk), lambda i, j, k: (i, jnp.minimum(k, nka - 1)))]
    if nx == 2:
        in_specs.append(spec((tm, tk), lambda i, j, k: (i, jnp.maximum(k - nka, 0))))
    if w_t:
        in_specs.append(spec((None, tn, tk), lambda i, j, k: (layer, j, k)))
    else:
        in_specs.append(spec((None, tk, tn), lambda i, j, k: (layer, k, j)))
    args = list(xs) + [w]
    if k_rem:
        assert not w_t
        rem_blk = (k_each - k_rem) // k_rem
        in_specs.append(spec((tm, k_rem), lambda i, j, k: (i, rem_blk)))
        in_specs.append(spec((None, k_rem, tn), lambda i, j, k: (layer, rem_blk, j)))
        args += [xs[0], w]
    if bias is not None:
        in_specs.append(spec((None, 1, tn), lambda i, j, k: (layer, 0, j)))
        args.append(bias)
    segments = None
    split_out = False
    out_specs = spec((tm, tn), lambda i, j, k: (i, j))
    out_shape = jax.ShapeDtypeStruct((m, n_out), out_dtype)
    if resid is not None:
        n_prompt_rows, tp = resid["bp"] * resid["tp"], resid["tp"]
        gate_blk = resid["gate_col"] // tn
        assert resid["gate_col"] % tn == 0 and resid["bp"] <= SUBLANES
        segments = [_row_segments(b * tm, tm, n_prompt_rows, tp) for b in range(m // tm)]
        ms = m - n_prompt_rows
        res_s_blk = resid["res_s_row0"] // ms
        gate_p_blk = ms // SUBLANES
        assert resid["res_s_row0"] % ms == 0 and ms % SUBLANES == 0
        in_specs += [
            spec((tm, tn), lambda i, j, k: (i, j)),
            spec((ms, tn), lambda i, j, k: (res_s_blk, j)),
            spec((SUBLANES, tn), lambda i, j, k: (gate_p_blk, gate_blk + j)),
            spec((ms, tn), lambda i, j, k: (0, gate_blk + j)),
        ]
        args += [resid["res_p"], resid["res_s"], resid["mod"], resid["mod"]]
        split_out = resid["split_out"]
        if split_out:
            out_specs = [out_specs, spec((ms, tn), lambda i, j, k: (0, j))]
            out_shape = [jax.ShapeDtypeStruct((n_prompt_rows, n_out), out_dtype),
                         jax.ShapeDtypeStruct((ms, n_out), out_dtype)]
    kern = functools.partial(_mm_kernel, nk=nk, nx=nx, nka=nka, w_t=w_t, has_rem=bool(k_rem), silu_x=silu_x,
                             has_bias=bias is not None, segments=segments, split_out=split_out)
    return pl.pallas_call(
        kern,
        grid=grid,
        in_specs=in_specs,
        out_specs=out_specs,
        out_shape=out_shape,
        scratch_shapes=[pltpu.VMEM((tm, tn), F32)],
        compiler_params=_cparams("parallel", "arbitrary", "arbitrary"),
        name=name,
    )(*args)


SWIGLU_SUB = 256


def _swiglu_kernel(x_ref, *refs, nk, nsub):
    wg = refs[:nsub]
    wu = refs[nsub:2 * nsub]
    o_ref = refs[2 * nsub]
    accg, accu = refs[2 * nsub + 1:]
    k = pl.program_id(2)

    def accum(first):
        x = x_ref[...]
        for s in range(nsub):
            cols = slice(s * SWIGLU_SUB, (s + 1) * SWIGLU_SUB)
            pg = _dot(x, wg[s][...].astype(BF16))
            pu = _dot(x, wu[s][...].astype(BF16))
            if first:
                accg[:, cols] = pg
                accu[:, cols] = pu
            else:
                accg[:, cols] += pg
                accu[:, cols] += pu

    pl.when(k == 0)(lambda: accum(True))
    pl.when(k > 0)(lambda: accum(False))

    @pl.when(k == nk - 1)
    def _finish():
        g = accg[...]
        o_ref[...] = (g * _sigmoid(g) * accu[...]).astype(o_ref.dtype)


def _swiglu_in(x, w, *, layer, tm, tk, nsub):
    m, kdim = x.shape
    f = w.shape[2] // 2
    sub = SWIGLU_SUB
    assert f % sub == 0 and m % tm == 0 and kdim % tk == 0
    nfb = f // sub
    last = 2 * nfb - 1
    nk = kdim // tk
    tn = nsub * sub
    grid = (m // tm, pl.cdiv(f, tn), nk)
    in_specs = [pl.BlockSpec((tm, tk), lambda i, j, k: (i, k))]
    for s in range(nsub):
        in_specs.append(pl.BlockSpec((None, tk, sub), lambda i, j, k, s=s: (layer, k, j * nsub + s)))
    for s in range(nsub):
        in_specs.append(pl.BlockSpec(
            (None, tk, sub), lambda i, j, k, s=s: (layer, k, jnp.minimum(nfb + j * nsub + s, last))))
    return pl.pallas_call(
        functools.partial(_swiglu_kernel, nk=nk, nsub=nsub),
        grid=grid,
        in_specs=in_specs,
        out_specs=pl.BlockSpec((tm, tn), lambda i, j, k: (i, j)),
        out_shape=jax.ShapeDtypeStruct((m, f), BF16),
        scratch_shapes=[pltpu.VMEM((tm, tn), F32), pltpu.VMEM((tm, tn), F32)],
        compiler_params=_cparams("parallel", "parallel", "arbitrary"),
        name="swiglu_in",
    )(x, *([w] * (2 * nsub)))


def _modulate_kernel(xp_ref, xs_ref, g_ref, shp_ref, scp_ref, shs_ref, scs_ref, o_ref, *, tr, tp, bp, npb):
    i = pl.program_id(0)
    group = 16

    def sweep(x_ref, shift_of, scale_of):
        def body(r, carry):
            rows = pl.ds(pl.multiple_of(r * group, group), group)
            y = _rms(x_ref[rows, :], g_ref[...])
            o_ref[rows, :] = (y * (1.0 + scale_of(rows)) + shift_of(rows)).astype(o_ref.dtype)
            return carry
        lax.fori_loop(0, tr // group, body, 0)

    @pl.when(i < npb)
    def _prompt_rows():
        seq = jnp.minimum(i * tr // tp, bp - 1)
        shift = shp_ref[pl.ds(seq, 1), :]
        scale = scp_ref[pl.ds(seq, 1), :]
        sweep(xp_ref, lambda rows: shift, lambda rows: scale)

    @pl.when(i >= npb)
    def _sample_rows():
        sweep(xs_ref, lambda rows: shs_ref[rows, :], lambda rows: scs_ref[rows, :])


def _modulate(x_p, x_s, s_row0, g, mod, *, ms, layer, tr, bp, tp):
    d = x_p.shape[1]
    npb = bp * tp // tr
    m = bp * tp + ms
    assert s_row0 % tr == 0 and ms % SUBLANES == 0
    srow = lambda i: jnp.maximum(i - npb, 0)
    p_blk = ms // SUBLANES
    return pl.pallas_call(
        functools.partial(_modulate_kernel, tr=tr, tp=tp, bp=bp, npb=npb),
        grid=(m // tr,),
        in_specs=[
            pl.BlockSpec((tr, d), lambda i: (jnp.minimum(i, npb - 1), 0)),
            pl.BlockSpec((tr, d), lambda i: (s_row0 // tr + srow(i), 0)),
            pl.BlockSpec((None, 1, d), lambda i: (layer, 0, 0)),
            pl.BlockSpec((SUBLANES, d), lambda i: (p_blk, 0)),
            pl.BlockSpec((SUBLANES, d), lambda i: (p_blk, 1)),
            pl.BlockSpec((tr, d), lambda i: (srow(i), 0)),
            pl.BlockSpec((tr, d), lambda i: (srow(i), 1)),
        ],
        out_specs=pl.BlockSpec((tr, d), lambda i: (i, 0)),
        out_shape=jax.ShapeDtypeStruct((m, d), BF16),
        compiler_params=_cparams("parallel"),
        name="modulate",
    )(x_p, x_s, g, mod, mod, mod, mod)


def _gmlp_kernel(u_ref, v_ref, lng_ref, lnb_ref, w_ref, b_ref, a_ref, vout_ref, *, groups, gdim):
    for g in range(groups):
        cols = slice(g * gdim, (g + 1) * gdim)
        vg = _gelu(v_ref[:, cols])
        mu = jnp.mean(vg, axis=-1, keepdims=True)
        xc = vg - mu
        var = jnp.mean(xc * xc, axis=-1, keepdims=True)
        y = xc * lax.rsqrt(var + EPS) * lng_ref[:, cols] + lnb_ref[:, cols]
        vout_ref[:, cols] = y
        mixed = _dot(w_ref[g].astype(BF16), y.astype(BF16)) + b_ref[:, g:g + 1]
        a_ref[:, cols] = (_gelu(u_ref[:, cols]) * mixed).astype(a_ref.dtype)


def _gmlp(z, ln_g, ln_b, w_mix, b_mix, *, layer, gw, n_prompt_blocks):
    m = z.shape[0]
    groups, rows = w_mix.shape[1], w_mix.shape[2]
    assert m % rows == 0
    which = lambda i: jnp.where(i < n_prompt_blocks, 0, 1)
    return pl.pallas_call(
        functools.partial(_gmlp_kernel, groups=groups, gdim=gw // groups),
        grid=(m // rows,),
        in_specs=[
            pl.BlockSpec((rows, gw), lambda i: (i, 0)),
            pl.BlockSpec((rows, gw), lambda i: (i, 1)),
            pl.BlockSpec((None, 1, gw), lambda i: (layer, 0, 0)),
            pl.BlockSpec((None, 1, gw), lambda i: (layer, 0, 0)),
            pl.BlockSpec((None, groups, rows, rows), lambda i: (which(i), 0, 0, 0)),
            pl.BlockSpec((None, rows, groups), lambda i: (which(i), 0, 0)),
        ],
        out_specs=[pl.BlockSpec((rows, gw), lambda i: (i, 0)), pl.BlockSpec((rows, gw), lambda i: (i, 0))],
        out_shape=[jax.ShapeDtypeStruct((m, gw), BF16), jax.ShapeDtypeStruct((m, gw), F32)],
        compiler_params=_cparams("parallel"),
        name="gmlp",
    )(z, z, ln_g, ln_b, w_mix, b_mix)


def _rope_rows(x, cosf, sinf):
    half = x.shape[-1] // 2
    swapped = jnp.concatenate([x[:, half:], x[:, :half]], axis=-1)
    return x * cosf + swapped * sinf


def _q_kernel(cq_ref, gqa_ref, wuq_ref, gqk_ref, cos_ref, sin_ref, wuk_ref, o_ref, cqn_sc,
              *, hb, nope, rank, scale):
    @pl.when(pl.program_id(1) == 0)
    def _norm():
        cqn_sc[...] = _rms(cq_ref[...], gqa_ref[...]).astype(BF16)

    gain = gqk_ref[...] * scale

    def project(h, _):
        return _dot(cqn_sc[...], wuq_ref[h].astype(BF16))

    def normalise(h, q):
        q = _rms(q, gain)
        o_ref[h, :, rank:] = _rope_rows(q[:, nope:], cos_ref[...], sin_ref[...]).astype(o_ref.dtype)
        return q[:, :nope].astype(BF16)

    def absorb(h, q_nope):
        o_ref[h, :, :rank] = _dot(q_nope, wuk_ref[h].astype(BF16)).astype(o_ref.dtype)

    _skewed(hb, [project, normalise, absorb])


def _mla_q(z, g_qa, w_uq_h, g_qk, cosf, sinf, w_uk, *, layer, cq_off, tm, hb, scale):
    m = z.shape[0]
    qr = g_qa.shape[-1]
    heads, _, qd = w_uq_h.shape
    nope, rank = w_uk.shape[2], w_uk.shape[3]
    rope_d = qd - nope
    assert cq_off % qr == 0 and heads % hb == 0
    cq_blk = cq_off // qr
    return pl.pallas_call(
        functools.partial(_q_kernel, hb=hb, nope=nope, rank=rank, scale=scale),
        grid=(m // tm, heads // hb),
        in_specs=[
            pl.BlockSpec((tm, qr), lambda i, h: (i, cq_blk)),
            pl.BlockSpec((None, 1, qr), lambda i, h: (layer, 0, 0)),
            pl.BlockSpec((hb, qr, qd), lambda i, h: (h, 0, 0)),
            pl.BlockSpec((None, 1, qd), lambda i, h: (layer, 0, 0)),
            pl.BlockSpec((tm, rope_d), lambda i, h: (i, 0)),
            pl.BlockSpec((tm, rope_d), lambda i, h: (i, 0)),
            pl.BlockSpec((None, hb, nope, rank), lambda i, h: (layer, h, 0, 0)),
        ],
        out_specs=pl.BlockSpec((hb, tm, rank + rope_d), lambda i, h: (h, i, 0)),
        out_shape=jax.ShapeDtypeStruct((heads, m, rank + rope_d), BF16),
        scratch_shapes=[pltpu.VMEM((tm, qr), BF16)],
        compiler_params=_cparams("parallel", "arbitrary"),
        name="mla_q",
    )(z, g_qa, w_uq_h, g_qk, cosf, sinf, w_uk)


def _kv_kernel(ckv_ref, kr_ref, gkv_ref, gkr_ref, cos_ref, sin_ref, ckv_o, kr_o, kcat_o,
               *, rank, rope_d):
    c = _rms(ckv_ref[...], gkv_ref[...])
    r = _rope_rows(_rms(kr_ref[:, :rope_d], gkr_ref[...]), cos_ref[...], sin_ref[...])
    ckv_o[...] = c
    kr_o[...] = r
    kcat_o[:, :rank] = c.astype(kcat_o.dtype)
    kcat_o[:, rank:] = r.astype(kcat_o.dtype)


def _mla_kv(z, g_kv, g_kr, cosf, sinf, *, layer, ckv_off, kr_off, tm):
    m = z.shape[0]
    rank = g_kv.shape[-1]
    rope_d = g_kr.shape[-1]
    assert ckv_off % rank == 0 and kr_off % LANES == 0
    return pl.pallas_call(
        functools.partial(_kv_kernel, rank=rank, rope_d=rope_d),
        grid=(m // tm,),
        in_specs=[
            pl.BlockSpec((tm, rank), lambda i: (i, ckv_off // rank)),
            pl.BlockSpec((tm, LANES), lambda i: (i, kr_off // LANES)),
            pl.BlockSpec((None, 1, rank), lambda i: (layer, 0, 0)),
            pl.BlockSpec((None, 1, rope_d), lambda i: (layer, 0, 0)),
            pl.BlockSpec((tm, rope_d), lambda i: (i, 0)),
            pl.BlockSpec((tm, rope_d), lambda i: (i, 0)),
        ],
        out_specs=[
            pl.BlockSpec((tm, rank), lambda i: (i, 0)),
            pl.BlockSpec((tm, rope_d), lambda i: (i, 0)),
            pl.BlockSpec((tm, rank + rope_d), lambda i: (i, 0)),
        ],
        out_shape=[
            jax.ShapeDtypeStruct((m, rank), F32),
            jax.ShapeDtypeStruct((m, rope_d), F32),
            jax.ShapeDtypeStruct((m, rank + rope_d), BF16),
        ],
        compiler_params=_cparams("parallel"),
        name="mla_kv",
    )(z, z, g_kv, g_kr, cosf, sinf)


def _pattn_kernel(q_ref, k_ref, wuv_ref, prev_ref, o_ref, m_sc, l_sc, acc_sc, *, heads, tq, rank, vd):
    del prev_ref
    qi = pl.program_id(2)

    m_sc[...] = jnp.full_like(m_sc, -jnp.inf)
    l_sc[...] = jnp.zeros_like(l_sc)
    acc_sc[...] = jnp.zeros_like(acc_sc)

    def step(kb, masked):
        kblk = k_ref[pl.ds(pl.multiple_of(kb * tq, tq), tq), :]
        vblk = kblk[:, :rank]
        if masked:
            visible = (lax.broadcasted_iota(jnp.int32, (tq, tq), 1)
                       <= lax.broadcasted_iota(jnp.int32, (tq, tq), 0))
        def scores(h, _):
            s = _dot_nt(q_ref[h], kblk)
            return jnp.where(visible, s, -jnp.inf) if masked else s

        def softmax(h, s):
            rs = slice(h * tq, (h + 1) * tq)
            m_prev = m_sc[rs, :]
            m_new = jnp.maximum(m_prev, jnp.max(s, axis=-1, keepdims=True))
            alpha = jnp.exp(m_prev - m_new)
            p = jnp.exp(s - m_new)
            l_sc[rs, :] = alpha * l_sc[rs, :] + jnp.sum(p, axis=-1, keepdims=True)
            m_sc[rs, :] = m_new
            return alpha, p.astype(BF16)

        def values(h, ap):
            rs = slice(h * tq, (h + 1) * tq)
            acc_sc[rs, :] = ap[0] * acc_sc[rs, :] + _dot(ap[1], vblk)

        _skewed(heads, [scores, softmax, values])

    def body(kb, carry):
        step(kb, False)
        return carry

    lax.fori_loop(0, qi, body, 0)
    step(qi, True)
    for h in range(heads):
        rs = slice(h * tq, (h + 1) * tq)
        o_lat = (acc_sc[rs, :] / l_sc[rs, :]).astype(BF16)
        o_ref[:, h * vd:(h + 1) * vd] = _dot(o_lat, wuv_ref[h].astype(BF16)).astype(o_ref.dtype)


def _prompt_attention(qcat, kcat, w_uv, *, layer, bp, tp, tq):
    heads, m, qd = qcat.shape
    rank, vd = w_uv.shape[2], w_uv.shape[3]
    nq = tp // tq
    hg = _div_tile(heads, ATTN_HEADS_PER_STEP, 1)
    return pl.pallas_call(
        functools.partial(_pattn_kernel, heads=hg, tq=tq, rank=rank, vd=vd),
        grid=(bp, heads // hg, nq),
        in_specs=[
            pl.BlockSpec((hg, tq, qd), lambda b, g, qi: (g, b * nq + qi, 0)),
            pl.BlockSpec((tp, qd), lambda b, g, qi: (b, 0)),
            pl.BlockSpec((None, hg, rank, vd), lambda b, g, qi: (layer, g, 0, 0)),
            pl.BlockSpec(memory_space=pl.ANY),
        ],
        out_specs=pl.BlockSpec((tq, hg * vd), lambda b, g, qi: (b * nq + qi, g)),
        out_shape=jax.ShapeDtypeStruct((m, heads * vd), BF16),
        input_output_aliases={3: 0},
        scratch_shapes=[pltpu.VMEM((hg * tq, 1), F32), pltpu.VMEM((hg * tq, 1), F32),
                        pltpu.VMEM((hg * tq, rank), F32)],
        compiler_params=_cparams("parallel", "parallel", "arbitrary"),
        name="prompt_attn",
    )(qcat, kcat, w_uv, jnp.zeros((m, heads * vd), BF16))


def _sattn_kernel(pt_ref, q_ref, knew_ref, ckv_hbm, krt_hbm, o_ref, ckv_ring, krt_ring, kbuf, krt_buf, s_sc, sems,
                  m_sc, l_sc, acc_sc, *, layer, pb, page, rank, nchunks, total, t_new):
    b = pl.program_id(0)
    slots = ATTN_RING_SLOTS
    ahead = slots - 1

    def chunk_copies(page_ids, slot):
        copies = []
        for i, pid in enumerate(page_ids):
            copies.append(pltpu.make_async_copy(
                ckv_hbm.at[layer, pid], ckv_ring.at[slot, pl.ds(i * page, page), :], sems.at[slot]))
            copies.append(pltpu.make_async_copy(krt_hbm.at[layer, pid], krt_ring.at[slot, i], sems.at[slot]))
        return copies

    def start_chunk(g, slot):
        src = lax.rem(g, total)
        for cp in chunk_copies([pt_ref[src * pb + i] for i in range(pb)], slot):
            cp.start()

    def wait_chunk(slot):
        for cp in chunk_copies([0] * pb, slot):
            cp.wait()

    slot_of = lambda c: c % slots

    @pl.when(b == 0)
    def _prime():
        for g0 in range(ahead):
            start_chunk(g0, slot_of(g0))

    m_sc[...] = jnp.full_like(m_sc, -jnp.inf)
    l_sc[...] = jnp.zeros_like(l_sc)
    acc_sc[...] = jnp.zeros_like(acc_sc)
    q = q_ref[...]

    def online_update(s, values):
        m_prev = m_sc[...]
        m_new = jnp.maximum(m_prev, jnp.max(s, axis=-1, keepdims=True))
        alpha = jnp.exp(m_prev - m_new)
        p = jnp.exp(s - m_new)
        l_sc[...] = alpha * l_sc[...] + jnp.sum(p, axis=-1, keepdims=True)
        acc_sc[...] = alpha * acc_sc[...] + _dot(p.astype(BF16), values)
        m_sc[...] = m_new

    def stage_scores(c):
        slot, par = slot_of(c), c % 2
        wait_chunk(slot)
        start_chunk(b * nchunks + c + ahead, slot_of(c + ahead))
        kbuf[par] = ckv_ring[slot].astype(BF16)
        for i in range(pb):
            krt_buf[par, :, i * page:(i + 1) * page] = krt_ring[slot, i].astype(BF16)
        s_sc[par] = _dot_nt(q[:, :rank], kbuf[par]) + _dot(q[:, rank:], krt_buf[par])

    def consume(c):
        online_update(s_sc[c % 2], kbuf[c % 2])

    stage_scores(0)
    for c in range(nchunks):
        if c + 1 < nchunks:
            stage_scores(c + 1)
        consume(c)

    k_new = knew_ref[...]
    s_new = _dot_nt(q, k_new)
    row = lax.rem(lax.broadcasted_iota(jnp.int32, s_new.shape, 0), t_new)
    col = lax.broadcasted_iota(jnp.int32, s_new.shape, 1)
    online_update(jnp.where(col <= row, s_new, -jnp.inf), k_new[:, :rank])
    o_ref[...] = (acc_sc[...] / l_sc[...]).astype(o_ref.dtype)

    @pl.when(b == pl.num_programs(0) - 1)
    def _drain():
        for g1 in range(ahead):
            wait_chunk(slot_of(total + g1))


def _sample_attention(q_s, k_new, cache_ckv, cache_krope_t, page_table, *, layer, t_new):
    b, rows, qd = q_s.shape
    page, rank = cache_ckv.shape[2], cache_ckv.shape[3]
    rope_d = cache_krope_t.shape[2]
    n_pages = page_table.shape[1]
    pb = _div_tile(n_pages, PAGES_PER_STEP, 1)
    nchunks = n_pages // pb
    total = b * nchunks
    assert nchunks % ATTN_RING_SLOTS == 0
    grid_spec = pltpu.PrefetchScalarGridSpec(
        num_scalar_prefetch=1,
        grid=(b,),
        in_specs=[
            pl.BlockSpec((None, rows, qd), lambda bi, pt: (bi, 0, 0)),
            pl.BlockSpec((None, k_new.shape[1], qd), lambda bi, pt: (bi, 0, 0)),
            pl.BlockSpec(memory_space=pl.ANY),
            pl.BlockSpec(memory_space=pl.ANY),
        ],
        out_specs=pl.BlockSpec((None, rows, rank), lambda bi, pt: (bi, 0, 0)),
        scratch_shapes=[pltpu.VMEM((ATTN_RING_SLOTS, pb * page, rank), cache_ckv.dtype),
                        pltpu.VMEM((ATTN_RING_SLOTS, pb, rope_d, page), cache_krope_t.dtype),
                        pltpu.VMEM((2, pb * page, rank), BF16), pltpu.VMEM((2, rope_d, pb * page), BF16),
                        pltpu.VMEM((2, rows, pb * page), F32),
                        pltpu.SemaphoreType.DMA((ATTN_RING_SLOTS,)),
                        pltpu.VMEM((rows, 1), F32), pltpu.VMEM((rows, 1), F32),
                        pltpu.VMEM((rows, rank), F32)],
    )
    return pl.pallas_call(
        functools.partial(_sattn_kernel, layer=layer, pb=pb, page=page, rank=rank, nchunks=nchunks,
                          total=total, t_new=t_new),
        grid_spec=grid_spec,
        out_shape=jax.ShapeDtypeStruct((b, rows, rank), BF16),
        compiler_params=_cparams("arbitrary"),
        name="sample_attn",
    )(page_table.reshape(-1), q_s, k_new, cache_ckv, cache_krope_t)


def _uv_kernel(x_ref, w_ref, prev_ref, o_ref):
    del prev_ref
    o_ref[...] = _dot(x_ref[...], w_ref[...].astype(BF16)).astype(o_ref.dtype)


def _value_up(o_lat, w_uv, o_prev, *, layer, row0):
    heads, ms, rank = o_lat.shape
    vd = w_uv.shape[3]
    assert row0 % ms == 0
    return pl.pallas_call(
        _uv_kernel,
        grid=(heads,),
        in_specs=[
            pl.BlockSpec((None, ms, rank), lambda h: (h, 0, 0)),
            pl.BlockSpec((None, None, rank, vd), lambda h: (layer, h, 0, 0)),
            pl.BlockSpec(memory_space=pl.ANY),
        ],
        out_specs=pl.BlockSpec((ms, vd), lambda h: (row0 // ms, h)),
        out_shape=jax.ShapeDtypeStruct(o_prev.shape, o_prev.dtype),
        input_output_aliases={2: 0},
        compiler_params=_cparams("parallel"),
        name="value_up",
    )(o_lat, w_uv, o_prev)


def _mlstm_kernel(*refs, heads, dk, dv, rows, sub_rows, nsub, aliased):
    (q_ref, k_ref, v_ref, o_ref, gc_ref, gr_ref, bgc_ref, bgr_ref, gmh_ref, c0_ref, n0_ref, m0_ref) = refs[:12]
    h_ref, c_ref, n_ref, m_ref = refs[-4:]
    sub = pl.program_id(2)

    @pl.when(pl.program_id(1) == 0)
    def _load_state():
        c_ref[...] = c0_ref[...]
        n_ref[...] = n0_ref[...]
        m_ref[...] = m0_ref[...]

    if nsub > 1:
        @pl.when(sub == 0)
        def _clear_rows():
            h_ref[...] = jnp.zeros_like(h_ref)

    n_all = n_ref[...]
    m_all = m_ref[...]
    n_out = n_all
    m_out = m_all
    head_id = lax.broadcasted_iota(jnp.int32, (heads, 1), 0)

    g_col = GATE_CAP * jnp.tanh((gc_ref[:, :2 * heads] + bgc_ref[...]) / GATE_CAP)
    g_row = GATE_CAP * jnp.tanh((gr_ref[...] + bgr_ref[...]) / GATE_CAP)

    def log_sigmoid(x):
        return jnp.minimum(x, 0.0) - jnp.log(1.0 + jnp.exp(-jnp.abs(x)))

    li = lax.broadcasted_iota(jnp.int32, (rows, rows), 0)
    si = lax.broadcasted_iota(jnp.int32, (rows, rows), 1)
    causal = si <= li
    ri = lax.broadcasted_iota(jnp.int32, (rows, 1), 0)
    ci = lax.broadcasted_iota(jnp.int32, (1, rows), 1)
    lo, hi = sub * sub_rows, (sub + 1) * sub_rows
    row_ok = (ri >= lo) & (ri < hi)
    col_ok = (ci >= lo) & (ci < hi)
    k_scale = dk ** -0.5

    state_rows = []

    def gates(h, _):
        ig_c = jnp.where(row_ok, g_col[:, h:h + 1], -jnp.inf)
        lf_c = jnp.where(row_ok, log_sigmoid(g_col[:, heads + h:heads + h + 1]), 0.0)
        ig_r = jnp.where(col_ok, g_row[h:h + 1, :], -jnp.inf)
        lf_r = jnp.where(col_ok, log_sigmoid(g_row[heads + h:heads + h + 1, :]), 0.0)
        b_c = jnp.sum(jnp.where(causal, lf_r, 0.0), axis=1, keepdims=True)
        b_r = jnp.sum(jnp.where(li <= si, lf_c, 0.0), axis=0, keepdims=True)
        b_last = jnp.sum(lf_r, axis=1, keepdims=True)
        m0 = m_all[h:h + 1, :]
        dmat = jnp.where(causal, b_c - b_r + ig_r, -jnp.inf)
        inter = b_c + m0
        m_pos = jnp.maximum(inter, jnp.max(dmat, axis=1, keepdims=True))
        g_r = b_last - b_r + ig_r
        m_new = jnp.maximum(b_last + m0, jnp.max(g_r, axis=1, keepdims=True))
        return dict(decay_w=jnp.exp(dmat - m_pos), a=jnp.exp(inter - m_pos), floor=jnp.exp(-m_pos),
                    m_new=m_new, decay=jnp.exp(b_last + m0 - m_new), wk_c=jnp.exp(b_last - b_c + ig_c - m_new))

    def products(h, st):
        qb = q_ref[:, h * dk:(h + 1) * dk].astype(BF16)
        kh = k_ref[:, h * dk:(h + 1) * dk] * k_scale
        st["qk"] = _dot_nt(qb, kh.astype(BF16))
        st["qc"] = _dot(qb, c_ref[h].astype(BF16))
        return st

    def update(h, st):
        qh = q_ref[:, h * dk:(h + 1) * dk]
        kh = k_ref[:, h * dk:(h + 1) * dk] * k_scale
        vb = v_ref[:, h * dv:(h + 1) * dv].astype(BF16)
        n0 = n_all[h:h + 1, :]
        w = st["decay_w"] * st["qk"]
        st["num"] = st["a"] * st["qc"] + _dot(w.astype(BF16), vb)
        st["nq"] = st["a"] * jnp.sum(qh * n0, axis=1, keepdims=True) + jnp.sum(w, axis=1, keepdims=True)
        kw = st["wk_c"] * kh
        c_ref[h] = st["decay"] * c_ref[h] + _dot_tn(kw.astype(BF16), vb)
        state_rows.append((st["decay"] * n0 + jnp.sum(kw, axis=0, keepdims=True), st["m_new"]))
        return st

    def emit(h, st):
        hs = st["num"] / jnp.maximum(jnp.abs(st["nq"]), st["floor"])
        hn = _rms(hs, gmh_ref[:, h * dv:(h + 1) * dv])
        out = (_sigmoid(o_ref[:, h * dv:(h + 1) * dv]) * hn).astype(h_ref.dtype)
        cols = slice(h * dv, (h + 1) * dv)
        if nsub == 1:
            h_ref[:, cols] = out
        else:
            h_ref[:, cols] = jnp.where(row_ok, out, h_ref[:, cols])

    _skewed(heads, [gates, products, update, emit])
    for h, (n_new, m_new) in enumerate(state_rows):
        n_out = jnp.where(head_id == h, n_new, n_out)
        m_out = jnp.where(head_id == h, m_new, m_out)
    n_ref[...] = n_out
    m_ref[...] = m_out


def _mlstm(z, g_rows, b_gates_c, b_gates_r, g_mh, c0, n0, m0, h_prev, *, layer, nb, nc, nsub,
           rows, sub_rows, row_blk0):
    m = z.shape[0]
    _, heads, dk, dv = c0.shape
    qw, vw = heads * dk, heads * dv
    assert (2 * qw) % vw == 0 and (2 * qw + 2 * vw) % LANES == 0
    g_blk = (2 * qw + 2 * vw) // LANES
    blk = lambda b, c, s: row_blk0 + b * nc + c
    seq = lambda b, c, s: b * nsub + s
    in_specs = [
        pl.BlockSpec((rows, qw), lambda b, c, s: (blk(b, c, s), 0)),
        pl.BlockSpec((rows, qw), lambda b, c, s: (blk(b, c, s), 1)),
        pl.BlockSpec((rows, vw), lambda b, c, s: (blk(b, c, s), 2 * qw // vw)),
        pl.BlockSpec((rows, vw), lambda b, c, s: (blk(b, c, s), 2 * qw // vw + 1)),
        pl.BlockSpec((rows, LANES), lambda b, c, s: (blk(b, c, s), g_blk)),
        pl.BlockSpec((None, 2 * heads, rows), lambda b, c, s: (b * nc + c, 0, 0)),
        pl.BlockSpec((None, 1, 2 * heads), lambda b, c, s: (layer, 0, 0)),
        pl.BlockSpec((None, 2 * heads, 1), lambda b, c, s: (layer, 0, 0)),
        pl.BlockSpec((None, 1, vw), lambda b, c, s: (layer, 0, 0)),
        pl.BlockSpec((None, heads, dk, dv), lambda b, c, s: (seq(b, c, s), 0, 0, 0)),
        pl.BlockSpec((None, heads, dk), lambda b, c, s: (seq(b, c, s), 0, 0)),
        pl.BlockSpec((None, heads, 1), lambda b, c, s: (seq(b, c, s), 0, 0)),
    ]
    args = [z, z, z, z, z, g_rows, b_gates_c, b_gates_r, g_mh, c0, n0, m0]
    aliases = {}
    if h_prev is not None:
        in_specs.append(pl.BlockSpec(memory_space=pl.ANY))
        args.append(h_prev)
        aliases = {12: 0}
    return pl.pallas_call(
        functools.partial(_mlstm_kernel, heads=heads, dk=dk, dv=dv, rows=rows, sub_rows=sub_rows,
                          nsub=nsub, aliased=h_prev is not None),
        grid=(nb, nc, nsub),
        in_specs=in_specs,
        out_specs=[
            pl.BlockSpec((rows, vw), lambda b, c, s: (blk(b, c, s), 0)),
            pl.BlockSpec((None, heads, dk, dv), lambda b, c, s: (seq(b, c, s), 0, 0, 0)),
            pl.BlockSpec((None, heads, dk), lambda b, c, s: (seq(b, c, s), 0, 0)),
            pl.BlockSpec((None, heads, 1), lambda b, c, s: (seq(b, c, s), 0, 0)),
        ],
        out_shape=[
            jax.ShapeDtypeStruct((m, vw), BF16),
            jax.ShapeDtypeStruct((nb * nsub, heads, dk, dv), F32),
            jax.ShapeDtypeStruct((nb * nsub, heads, dk), F32),
            jax.ShapeDtypeStruct((nb * nsub, heads, 1), F32),
        ],
        input_output_aliases=aliases,
        compiler_params=_cparams("parallel", "arbitrary", "arbitrary"),
        name="mlstm",
    )(*args)


def _rope_tables(pos, rope_d):
    half = rope_d // 2
    inv = ROPE_THETA ** (-jnp.arange(half, dtype=F32) * (2.0 / rope_d))
    ang = pos.astype(F32)[:, None] * inv[None, :]
    cos, sin = jnp.cos(ang), jnp.sin(ang)
    return jnp.concatenate([cos, cos], axis=-1), jnp.concatenate([-sin, sin], axis=-1)


def kernel(x_prompt, x_sample, c_prompt, c_sample, cache_ckv, cache_krope, page_table,
           state_mlstm_C, state_mlstm_n, state_mlstm_m,
           norm_mix_g, ada_mix_w, ada_mix_b, norm_ffn_g, ada_ffn_w, ada_ffn_b, ffn_w_in, ffn_w_out,
           ev_w_in, ev_ln_v_g, ev_ln_v_b, ev_w_s, ev_b_s, ev_g_qa, ev_w_uq, ev_g_qk, ev_g_kv, ev_g_kr,
           ev_w_uk, ev_w_uv, ev_w_out, od_w_in, od_b_gates, od_g_mh, od_w_out):
    bp, tp, d = x_prompt.shape
    bs, ts, _ = x_sample.shape
    depth = norm_mix_g.shape[0]
    mp, ms = bp * tp, bs * ts
    m = mp + ms
    past = page_table.shape[1] * cache_ckv.shape[2]

    cond_rows = ms + 16
    assert bp <= SUBLANES
    c_all = jnp.concatenate([jnp.repeat(c_sample, ts, axis=0), c_prompt,
                             jnp.zeros((cond_rows - ms - bp, d), F32)], axis=0)

    def ada(w, b, l):
        return _matmul([c_all], w, layer=l, n_out=3 * d, tm=cond_rows, tn=_div_tile(3 * d, 1024, LANES),
                       tk=_div_tile(d, 2048, LANES), out_dtype=F32, bias=b.reshape(depth, 1, 3 * d),
                       silu_x=True, name="ada")

    tm = _div_tile(m, MM_TM_PREF, 16)
    tr = _div_tile(ms, 256, 16)
    assert tp % tr == 0

    def residual(xs_in, w, l, x, mod, name, split_out=False):
        tk, k_rem = _k_tile(xs_in[0].shape[1], 1536)
        resid = dict(res_p=x[0], res_s=x[1], res_s_row0=x[2], mod=mod, gate_col=2 * d,
                     bp=bp, tp=tp, split_out=split_out)
        out = _matmul(xs_in, w, layer=l, n_out=d, tm=tm, tn=_div_tile(d, 512, LANES), tk=tk, k_rem=k_rem,
                      out_dtype=F32, resid=resid, name=name)
        return (out[0], out[1], 0) if split_out else (out, out, mp)

    gw = ev_ln_v_g.shape[-1]
    groups, gm_chunk = ev_w_s.shape[1], ev_w_s.shape[2]
    qr_rank = ev_g_qa.shape[-1]
    heads = ev_w_uk.shape[1]
    nope, rank = ev_w_uk.shape[2], ev_w_uk.shape[3]
    rope_d = ev_g_kr.shape[-1]
    qd = rank + rope_d
    ev_in = ev_w_in.shape[-1]
    cq_off, ckv_off, kr_off = 2 * gw, 2 * gw + qr_rank, 2 * gw + qr_rank + rank
    mla_scale = float(nope + rope_d) ** -0.5
    pos = jnp.concatenate([jnp.tile(jnp.arange(tp), bp), jnp.tile(past + jnp.arange(ts), bs)])
    cosf, sinf = _rope_tables(pos, rope_d)
    ev_w_in_t = jnp.swapaxes(ev_w_in, 1, 2)
    od_w_in_t = jnp.swapaxes(od_w_in, 1, 2)
    cache_krope_t = jnp.swapaxes(cache_krope, 2, 3)

    def mix_matrix(w_s, b_s, t):
        lc = min(t, gm_chunk)
        reps = gm_chunk // lc
        w = jnp.where(jnp.tril(jnp.ones((lc, lc), dtype=bool)), w_s[:, :lc, :lc], 0.0)
        w_blk = jnp.einsum("ab,gts->gatbs", jnp.eye(reps, dtype=F32), w).reshape(groups, gm_chunk, gm_chunk)
        b_blk = jnp.tile(b_s[:, :lc], (1, reps)).T
        return w_blk, b_blk

    ml_heads, ml_dk, ml_dv = state_mlstm_C.shape[2], state_mlstm_C.shape[3], state_mlstm_C.shape[4]
    lp = min(tp, ML_CHUNK)
    ncp = tp // lp
    ml_sub = ML_SAMPLE_ROWS // ts
    assert ML_SAMPLE_ROWS % ts == 0 and bs % ml_sub == 0 and mp % ML_SAMPLE_ROWS == 0

    x = (x_prompt.reshape(mp, d), x_sample.reshape(ms, d), 0)
    ckv_p, kr_p, ckv_s, kr_s, v_s = [], [], [], [], []
    cp_l, np_l, mp_l, cs_l, ns_l, ms_l = [], [], [], [], [], []
    for l in range(depth):
        j = l // 2
        mod = ada(ada_mix_w, ada_mix_b, l)
        h = _modulate(*x, norm_mix_g.reshape(depth, 1, d), mod, ms=ms, layer=l, tr=tr, bp=bp, tp=tp)
        if l % 2 == 0:
            z = _matmul([h], ev_w_in_t, layer=j, n_out=ev_in, tm=tm, tn=min(ev_in, 1024),
                        tk=_div_tile(d, 1024, LANES), w_t=True, out_dtype=F32, name="ev_in")
            w_p, b_p = mix_matrix(ev_w_s[j], ev_b_s[j], tp)
            w_smp, b_smp = mix_matrix(ev_w_s[j], ev_b_s[j], ts)
            a, v_ln = _gmlp(z, ev_ln_v_g.reshape(-1, 1, gw), ev_ln_v_b.reshape(-1, 1, gw),
                            jnp.stack([w_p, w_smp]), jnp.stack([b_p, b_smp]),
                            layer=j, gw=gw, n_prompt_blocks=mp // gm_chunk)
            tmq = _div_tile(ms, 512, 16)
            w_uq_h = ev_w_uq[j].reshape(qr_rank, heads, nope + rope_d).transpose(1, 0, 2)
            qcat = _mla_q(z, ev_g_qa.reshape(-1, 1, qr_rank), w_uq_h, ev_g_qk.reshape(-1, 1, nope + rope_d),
                          cosf, sinf, ev_w_uk, layer=j, cq_off=cq_off, tm=tmq, hb=_div_tile(heads, 4, 1),
                          scale=mla_scale)
            ckv_n, kr_n, kcat = _mla_kv(z, ev_g_kv.reshape(-1, 1, rank), ev_g_kr.reshape(-1, 1, rope_d),
                                        cosf, sinf, layer=j, ckv_off=ckv_off, kr_off=kr_off, tm=tmq)
            o = _prompt_attention(qcat, kcat, ev_w_uv, layer=j, bp=bp, tp=tp, tq=_div_tile(tp, ATTN_TQ_PREF, 16))
            q_s = qcat[:, mp:].reshape(heads, bs, ts, qd).transpose(1, 0, 2, 3).reshape(bs, heads * ts, qd)
            k_new = jnp.pad(kcat[mp:].reshape(bs, ts, qd), ((0, 0), (0, LANES - ts), (0, 0)))
            o_lat_s = _sample_attention(q_s, k_new, cache_ckv, cache_krope_t, page_table, layer=j, t_new=ts)
            o_lat_s = o_lat_s.reshape(bs, heads, ts, rank).transpose(1, 0, 2, 3).reshape(heads, ms, rank)
            o = _value_up(o_lat_s, ev_w_uv, o, layer=j, row0=mp)
            x = residual([a, o], ev_w_out, j, x, mod, "ev_out")
            ckv_p.append(ckv_n[:mp].reshape(bp, tp, rank))
            kr_p.append(kr_n[:mp].reshape(bp, tp, rope_d))
            ckv_s.append(ckv_n[mp:].reshape(bs, ts, rank))
            kr_s.append(kr_n[mp:].reshape(bs, ts, rope_d))
            v_s.append(v_ln[mp:].reshape(bs, ts, gw))
        else:
            od_in = od_w_in.shape[-1]
            gate_off = od_in - 2 * ml_heads
            bg_c = od_b_gates.reshape(-1, 1, 2 * ml_heads)
            bg_r = od_b_gates.reshape(-1, 2 * ml_heads, 1)
            gmh = od_g_mh.reshape(-1, 1, ml_heads * ml_dv)
            z = _matmul([h], od_w_in_t, layer=j, n_out=od_in, tm=tm, tn=min(od_in, 1024),
                        tk=_div_tile(d, 1024, LANES), w_t=True, out_dtype=F32, name="od_in")
            gates = z[:, gate_off:]
            g_rows_p = jnp.swapaxes(gates[:mp].reshape(bp * ncp, lp, 2 * ml_heads), 1, 2)
            g_rows_s = jnp.swapaxes(gates[mp:].reshape(ms // ML_SAMPLE_ROWS, ML_SAMPLE_ROWS, 2 * ml_heads), 1, 2)
            zeros = lambda *shape: jnp.zeros(shape, F32)
            hm, c_p, n_p, m_p = _mlstm(z, g_rows_p, bg_c, bg_r, gmh, zeros(bp, ml_heads, ml_dk, ml_dv),
                                       zeros(bp, ml_heads, ml_dk), zeros(bp, ml_heads, 1),
                                       jnp.zeros((m, ml_heads * ml_dv), BF16),
                                       layer=j, nb=bp, nc=ncp, nsub=1, rows=lp, sub_rows=lp, row_blk0=0)
            hm, c_s, n_s, m_s = _mlstm(z, g_rows_s, bg_c, bg_r, gmh, state_mlstm_C[j], state_mlstm_n[j],
                                       state_mlstm_m[j].reshape(bs, ml_heads, 1), hm,
                                       layer=j, nb=bs // ml_sub, nc=1, nsub=ml_sub, rows=ML_SAMPLE_ROWS,
                                       sub_rows=ts, row_blk0=mp // ML_SAMPLE_ROWS)
            x = residual([hm], od_w_out, j, x, mod, "od_out")
            cp_l.append(c_p)
            np_l.append(n_p)
            mp_l.append(m_p.reshape(bp, ml_heads))
            cs_l.append(c_s)
            ns_l.append(n_s)
            ms_l.append(m_s.reshape(bs, ml_heads))

        mod = ada(ada_ffn_w, ada_ffn_b, l)
        h = _modulate(*x, norm_ffn_g.reshape(depth, 1, d), mod, ms=ms, layer=l, tr=tr, bp=bp, tp=tp)
        act = _swiglu_in(h, ffn_w_in, layer=l, tm=_div_tile(m, SWIGLU_TM_PREF, 16),
                         tk=_div_tile(d, 512, LANES), nsub=2)
        x = residual([act], ffn_w_out, l, x, mod, "ffn_out", split_out=l == depth - 1)

    return (x[0].reshape(bp, tp, d), x[1].reshape(bs, ts, d),
            jnp.stack(ckv_p), jnp.stack(kr_p), jnp.stack(ckv_s), jnp.stack(kr_s), jnp.stack(v_s),
            jnp.stack(cp_l), jnp.stack(np_l), jnp.stack(mp_l),
            jnp.stack(cs_l), jnp.stack(ns_l), jnp.stack(ms_l))
```

```python
import functools

import jax
import jax.numpy as jnp
from jax import lax
from jax.experimental import pallas as pl
from jax.experimental.pallas import tpu as pltpu

EPS = 1e-6
ROPE_THETA = 10000.0
GATE_CAP = 15.0
ML_CHUNK = 64
ML_SAMPLE_ROWS = 16
PAGES_PER_STEP = 16
ATTN_RING_SLOTS = 4
LANES = 128
SUBLANES = 8
V7X_VMEM_LIMIT = 56 * 1024 * 1024
MM_TM_PREF = 2176
SWIGLU_TM_PREF = 4352
ATTN_TQ_PREF = 512
ATTN_HEADS_PER_STEP = 4

F32 = jnp.float32
BF16 = jnp.bfloat16


def _cparams(*sem):
    return pltpu.CompilerParams(dimension_semantics=sem, vmem_limit_bytes=V7X_VMEM_LIMIT)


def _div_tile(dim, pref, align):
    if dim <= pref:
        return dim
    t = (pref // align) * align
    while t >= align:
        if dim % t == 0:
            return t
        t -= align
    return dim


def _k_tile(kdim, pref):
    exact = _div_tile(kdim, pref, 256)
    if kdim <= pref or 2 * exact > pref:
        return exact, 0
    for tk in range((pref // 256) * 256, exact, -256):
        rem = kdim % tk
        if rem and (kdim - rem) % rem == 0 and rem % LANES == 0:
            return tk, rem
    return exact, 0


def _rms(x, g):
    return x * lax.rsqrt(jnp.mean(x * x, axis=-1, keepdims=True) + EPS) * g


def _gelu(x):
    return x * (0.5 * (1.0 + jnp.tanh(0.7978845608028654 * (x + 0.044715 * (x * x * x)))))


def _sigmoid(x):
    return 0.5 * (jnp.tanh(0.5 * x) + 1.0)


def _dot(a, b):
    return jnp.dot(a, b, preferred_element_type=F32)


def _dot_nt(a, b):
    return lax.dot_general(a, b, (((1,), (1,)), ((), ())), preferred_element_type=F32)


def _dot_tn(a, b):
    return lax.dot_general(a, b, (((0,), (0,)), ((), ())), preferred_element_type=F32)


def _skewed(n, stages):
    vals = [None] * n
    for t in range(n + len(stages) - 1):
        for s, fn in enumerate(stages):
            i = t - s
            if 0 <= i < n:
                vals[i] = fn(i, vals[i])
    return vals


def _row_segments(block_start, tm, n_prompt_rows, tp):
    segs, r, end = [], block_start, block_start + tm
    while r < end:
        if r < n_prompt_rows:
            seq = r // tp
            hi = min(end, (seq + 1) * tp)
            segs.append((r - block_start, hi - block_start, "p", seq))
        else:
            hi = end
            segs.append((r - block_start, hi - block_start, "s", r - n_prompt_rows))
        r = hi
    return segs


def _mm_kernel(*refs, nk, nx, nka, w_t, has_rem, silu_x, has_bias, segments, split_out):
    it = iter(refs)
    x_refs = [next(it) for _ in range(nx)]
    w_ref = next(it)
    xr_ref = next(it) if has_rem else None
    wr_ref = next(it) if has_rem else None
    b_ref = next(it) if has_bias else None
    resp_ref = ress_ref = gp_ref = gs_ref = None
    if segments is not None:
        resp_ref, ress_ref, gp_ref, gs_ref = next(it), next(it), next(it), next(it)
    o_ref = next(it)
    os_ref = next(it) if split_out else None
    acc_ref = next(it)
    silu_sc = next(it) if silu_x else None
    k = pl.program_id(2)

    def accum(x_ref, first):
        if silu_x:
            @pl.when(pl.program_id(0) == 0)
            def _activate():
                x = x_ref[...]
                silu_sc[k] = (x * _sigmoid(x)).astype(BF16)
            xb = silu_sc[k]
        else:
            xb = x_ref[...].astype(BF16)
        w = w_ref[...].astype(BF16)
        prod = _dot_nt(xb, w) if w_t else _dot(xb, w)
        if not first:
            acc_ref[...] += prod
        elif has_rem:
            acc_ref[...] = prod + _dot(xr_ref[...].astype(BF16), wr_ref[...].astype(BF16))
        else:
            acc_ref[...] = prod

    pl.when(k == 0)(lambda: accum(x_refs[0], True))
    if nx == 1:
        pl.when(k > 0)(lambda: accum(x_refs[0], False))
    else:
        pl.when((k > 0) & (k < nka))(lambda: accum(x_refs[0], False))
        pl.when(k >= nka)(lambda: accum(x_refs[1], False))

    @pl.when(k == nk - 1)
    def _finish():
        if segments is None:
            y = acc_ref[...]
            if has_bias:
                y = y + b_ref[...]
            o_ref[...] = y.astype(o_ref.dtype)
        else:
            i = pl.program_id(1)
            for bi, segs in enumerate(segments):
                @pl.when(i == bi)
                def _gated(segs=segs):
                    for lo, hi, kind, idx in segs:
                        if kind == "p":
                            o_ref[lo:hi, :] = resp_ref[lo:hi, :] + gp_ref[idx:idx + 1, :] * acc_ref[lo:hi, :]
                        else:
                            srows = slice(idx, idx + hi - lo)
                            y = ress_ref[srows, :] + gs_ref[srows, :] * acc_ref[lo:hi, :]
                            if split_out:
                                os_ref[srows, :] = y
                            else:
                                o_ref[lo:hi, :] = y


def _matmul(xs, w, *, layer, n_out, tm, tn, tk, out_dtype, w_t=False, k_rem=0, bias=None,
            silu_x=False, resid=None, name="mm"):
    m, k_each = xs[0].shape
    nx = len(xs)
    assert m % tm == 0 and (k_each - k_rem) % tk == 0 and (nx == 1 or k_rem == 0)
    nka = (k_each - k_rem) // tk
    nk = nka * nx
    grid = (pl.cdiv(n_out, tn), m // tm, nk)

    def spec(shape, index):
        return pl.BlockSpec(shape, lambda j, i, k: index(i, j, k))

    in_specs = [spec((tm, tk), lambda i, j, k: (i, jnp.minimum(k, nka - 1)))]
    if nx == 2:
        in_specs.append(spec((tm, tk), lambda i, j, k: (i, jnp.maximum(k - nka, 0))))
    if w_t:
        in_specs.append(spec((None, tn, tk), lambda i, j, k: (layer, j, k)))
    else:
        in_specs.append(spec((None, tk, tn), lambda i, j, k: (layer, k, j)))
    args = list(xs) + [w]
    if k_rem:
        assert not w_t
        rem_blk = (k_each - k_rem) // k_rem
        in_specs.append(spec((tm, k_rem), lambda i, j, k: (i, rem_blk)))
        in_specs.append(spec((None, k_rem, tn), lambda i, j, k: (layer, rem_blk, j)))
        args += [xs[0], w]
    if bias is not None:
        in_specs.append(spec((None, 1, tn), lambda i, j, k: (layer, 0, j)))
        args.append(bias)
    segments = None
    split_out = False
    out_specs = spec((tm, tn), lambda i, j, k: (i, j))
    out_shape = jax.ShapeDtypeStruct((m, n_out), out_dtype)
    if resid is not None:
        n_prompt_rows, tp = resid["bp"] * resid["tp"], resid["tp"]
        gate_blk = resid["gate_col"] // tn
        assert resid["gate_col"] % tn == 0 and resid["bp"] <= SUBLANES
        segments = [_row_segments(b * tm, tm, n_prompt_rows, tp) for b in range(m // tm)]
        ms = m - n_prompt_rows
        res_s_blk = resid["res_s_row0"] // ms
        gate_p_blk = ms // SUBLANES
        assert resid["res_s_row0"] % ms == 0 and ms % SUBLANES == 0
        in_specs += [
            spec((tm, tn), lambda i, j, k: (i, j)),
            spec((ms, tn), lambda i, j, k: (res_s_blk, j)),
            spec((SUBLANES, tn), lambda i, j, k: (gate_p_blk, gate_blk + j)),
            spec((ms, tn), lambda i, j, k: (0, gate_blk + j)),
        ]
        args += [resid["res_p"], resid["res_s"], resid["mod"], resid["mod"]]
        split_out = resid["split_out"]
        if split_out:
            out_specs = [out_specs, spec((ms, tn), lambda i, j, k: (0, j))]
            out_shape = [jax.ShapeDtypeStruct((n_prompt_rows, n_out), out_dtype),
                         jax.ShapeDtypeStruct((ms, n_out), out_dtype)]
    kern = functools.partial(_mm_kernel, nk=nk, nx=nx, nka=nka, w_t=w_t, has_rem=bool(k_rem), silu_x=silu_x,
                             has_bias=bias is not None, segments=segments, split_out=split_out)
    return pl.pallas_call(
        kern,
        grid=grid,
        in_specs=in_specs,
        out_specs=out_specs,
        out_shape=out_shape,
        scratch_shapes=[pltpu.VMEM((tm, tn), F32)] + ([pltpu.VMEM((nk, tm, tk), BF16)] if silu_x else []),
        compiler_params=_cparams("arbitrary" if silu_x else "parallel", "arbitrary", "arbitrary"),
        name=name,
    )(*args)


SWIGLU_SUB = 256


def _swiglu_kernel(x_ref, *refs, nk, nsub):
    wg = refs[:nsub]
    wu = refs[nsub:2 * nsub]
    o_ref = refs[2 * nsub]
    accg, accu = refs[2 * nsub + 1:]
    k = pl.program_id(2)

    def accum(first):
        x = x_ref[...]
        for s in range(nsub):
            cols = slice(s * SWIGLU_SUB, (s + 1) * SWIGLU_SUB)
            pg = _dot(x, wg[s][...].astype(BF16))
            pu = _dot(x, wu[s][...].astype(BF16))
            if first:
                accg[:, cols] = pg
                accu[:, cols] = pu
            else:
                accg[:, cols] += pg
                accu[:, cols] += pu

    pl.when(k == 0)(lambda: accum(True))
    pl.when(k > 0)(lambda: accum(False))

    @pl.when(k == nk - 1)
    def _finish():
        g = accg[...]
        o_ref[...] = (g * _sigmoid(g) * accu[...]).astype(o_ref.dtype)


def _swiglu_in(x, w, *, layer, tm, tk, nsub):
    m, kdim = x.shape
    f = w.shape[2] // 2
    sub = SWIGLU_SUB
    assert f % sub == 0 and m % tm == 0 and kdim % tk == 0
    nfb = f // sub
    last = 2 * nfb - 1
    nk = kdim // tk
    tn = nsub * sub
    grid = (m // tm, pl.cdiv(f, tn), nk)
    in_specs = [pl.BlockSpec((tm, tk), lambda i, j, k: (i, k))]
    for s in range(nsub):
        in_specs.append(pl.BlockSpec((None, tk, sub), lambda i, j, k, s=s: (layer, k, j * nsub + s)))
    for s in range(nsub):
        in_specs.append(pl.BlockSpec(
            (None, tk, sub), lambda i, j, k, s=s: (layer, k, jnp.minimum(nfb + j * nsub + s, last))))
    return pl.pallas_call(
        functools.partial(_swiglu_kernel, nk=nk, nsub=nsub),
        grid=grid,
        in_specs=in_specs,
        out_specs=pl.BlockSpec((tm, tn), lambda i, j, k: (i, j)),
        out_shape=jax.ShapeDtypeStruct((m, f), BF16),
        scratch_shapes=[pltpu.VMEM((tm, tn), F32), pltpu.VMEM((tm, tn), F32)],
        compiler_params=_cparams("parallel", "parallel", "arbitrary"),
        name="swiglu_in",
    )(x, *([w] * (2 * nsub)))


def _modulate_kernel(xp_ref, xs_ref, g_ref, shp_ref, scp_ref, shs_ref, scs_ref, o_ref, *, tr, tp, bp, npb):
    i = pl.program_id(0)
    group = 16

    def sweep(x_ref, shift_of, scale_of):
        def body(r, carry):
            rows = pl.ds(pl.multiple_of(r * group, group), group)
            y = _rms(x_ref[rows, :], g_ref[...])
            o_ref[rows, :] = (y * (1.0 + scale_of(rows)) + shift_of(rows)).astype(o_ref.dtype)
            return carry
        lax.fori_loop(0, tr // group, body, 0)

    @pl.when(i < npb)
    def _prompt_rows():
        seq = jnp.minimum(i * tr // tp, bp - 1)
        shift = shp_ref[pl.ds(seq, 1), :]
        scale = scp_ref[pl.ds(seq, 1), :]
        sweep(xp_ref, lambda rows: shift, lambda rows: scale)

    @pl.when(i >= npb)
    def _sample_rows():
        sweep(xs_ref, lambda rows: shs_ref[rows, :], lambda rows: scs_ref[rows, :])


def _modulate(x_p, x_s, s_row0, g, mod, *, ms, layer, tr, bp, tp):
    d = x_p.shape[1]
    npb = bp * tp // tr
    m = bp * tp + ms
    assert s_row0 % tr == 0 and ms % SUBLANES == 0
    srow = lambda i: jnp.maximum(i - npb, 0)
    p_blk = ms // SUBLANES
    return pl.pallas_call(
        functools.partial(_modulate_kernel, tr=tr, tp=tp, bp=bp, npb=npb),
        grid=(m // tr,),
        in_specs=[
            pl.BlockSpec((tr, d), lambda i: (jnp.minimum(i, npb - 1), 0)),
            pl.BlockSpec((tr, d), lambda i: (s_row0 // tr + srow(i), 0), pipeline_mode=pl.Buffered(1)),
            pl.BlockSpec((None, 1, d), lambda i: (layer, 0, 0)),
            pl.BlockSpec((SUBLANES, d), lambda i: (p_blk, 0)),
            pl.BlockSpec((SUBLANES, d), lambda i: (p_blk, 1)),
            pl.BlockSpec((tr, d), lambda i: (srow(i), 0), pipeline_mode=pl.Buffered(1)),
            pl.BlockSpec((tr, d), lambda i: (srow(i), 1), pipeline_mode=pl.Buffered(1)),
        ],
        out_specs=pl.BlockSpec((tr, d), lambda i: (i, 0)),
        out_shape=jax.ShapeDtypeStruct((m, d), BF16),
        compiler_params=_cparams("parallel"),
        name="modulate",
    )(x_p, x_s, g, mod, mod, mod, mod)


def _gmlp_kernel(u_ref, v_ref, lng_ref, lnb_ref, w_ref, b_ref, a_ref, vout_ref, *, groups, gdim):
    for g in range(groups):
        cols = slice(g * gdim, (g + 1) * gdim)
        vg = _gelu(v_ref[:, cols])
        mu = jnp.mean(vg, axis=-1, keepdims=True)
        xc = vg - mu
        var = jnp.mean(xc * xc, axis=-1, keepdims=True)
        y = xc * lax.rsqrt(var + EPS) * lng_ref[:, cols] + lnb_ref[:, cols]
        vout_ref[:, cols] = y
        mixed = _dot(w_ref[g].astype(BF16), y.astype(BF16)) + b_ref[:, g:g + 1]
        a_ref[:, cols] = (_gelu(u_ref[:, cols]) * mixed).astype(a_ref.dtype)


def _gmlp(z, ln_g, ln_b, w_mix, b_mix, *, layer, gw, n_prompt_blocks):
    m = z.shape[0]
    groups, rows = w_mix.shape[1], w_mix.shape[2]
    assert m % rows == 0
    which = lambda i: jnp.where(i < n_prompt_blocks, 0, 1)
    return pl.pallas_call(
        functools.partial(_gmlp_kernel, groups=groups, gdim=gw // groups),
        grid=(m // rows,),
        in_specs=[
            pl.BlockSpec((rows, gw), lambda i: (i, 0)),
            pl.BlockSpec((rows, gw), lambda i: (i, 1)),
            pl.BlockSpec((None, 1, gw), lambda i: (layer, 0, 0)),
            pl.BlockSpec((None, 1, gw), lambda i: (layer, 0, 0)),
            pl.BlockSpec((None, groups, rows, rows), lambda i: (which(i), 0, 0, 0)),
            pl.BlockSpec((None, rows, groups), lambda i: (which(i), 0, 0)),
        ],
        out_specs=[pl.BlockSpec((rows, gw), lambda i: (i, 0)), pl.BlockSpec((rows, gw), lambda i: (i, 0))],
        out_shape=[jax.ShapeDtypeStruct((m, gw), BF16), jax.ShapeDtypeStruct((m, gw), F32)],
        compiler_params=_cparams("parallel"),
        name="gmlp",
    )(z, z, ln_g, ln_b, w_mix, b_mix)


def _rope_rows(x, cosf, sinf):
    half = x.shape[-1] // 2
    swapped = jnp.concatenate([x[:, half:], x[:, :half]], axis=-1)
    return x * cosf + swapped * sinf


def _q_kernel(cq_ref, gqa_ref, wuq_ref, gqk_ref, cos_ref, sin_ref, wuk_ref, o_ref, cqn_sc,
              *, hb, nope, rank, scale):
    @pl.when(pl.program_id(1) == 0)
    def _norm():
        cqn_sc[...] = _rms(cq_ref[...], gqa_ref[...]).astype(BF16)

    gain = gqk_ref[...] * scale

    def project(h, _):
        return _dot(cqn_sc[...], wuq_ref[h].astype(BF16))

    def normalise(h, q):
        q = _rms(q, gain)
        o_ref[h, :, rank:] = _rope_rows(q[:, nope:], cos_ref[...], sin_ref[...]).astype(o_ref.dtype)
        return q[:, :nope].astype(BF16)

    def absorb(h, q_nope):
        o_ref[h, :, :rank] = _dot(q_nope, wuk_ref[h].astype(BF16)).astype(o_ref.dtype)

    _skewed(hb, [project, normalise, absorb])


def _mla_q(z, g_qa, w_uq_h, g_qk, cosf, sinf, w_uk, *, layer, cq_off, tm, hb, scale):
    m = z.shape[0]
    qr = g_qa.shape[-1]
    heads, _, qd = w_uq_h.shape
    nope, rank = w_uk.shape[2], w_uk.shape[3]
    rope_d = qd - nope
    assert cq_off % qr == 0 and heads % hb == 0
    cq_blk = cq_off // qr
    return pl.pallas_call(
        functools.partial(_q_kernel, hb=hb, nope=nope, rank=rank, scale=scale),
        grid=(m // tm, heads // hb),
        in_specs=[
            pl.BlockSpec((tm, qr), lambda i, h: (i, cq_blk)),
            pl.BlockSpec((None, 1, qr), lambda i, h: (layer, 0, 0)),
            pl.BlockSpec((hb, qr, qd), lambda i, h: (h, 0, 0)),
            pl.BlockSpec((None, 1, qd), lambda i, h: (layer, 0, 0)),
            pl.BlockSpec((tm, rope_d), lambda i, h: (i, 0)),
            pl.BlockSpec((tm, rope_d), lambda i, h: (i, 0)),
            pl.BlockSpec((None, hb, nope, rank), lambda i, h: (layer, h, 0, 0)),
        ],
        out_specs=pl.BlockSpec((hb, tm, rank + rope_d), lambda i, h: (h, i, 0)),
        out_shape=jax.ShapeDtypeStruct((heads, m, rank + rope_d), BF16),
        scratch_shapes=[pltpu.VMEM((tm, qr), BF16)],
        compiler_params=_cparams("parallel", "arbitrary"),
        name="mla_q",
    )(z, g_qa, w_uq_h, g_qk, cosf, sinf, w_uk)


def _kv_kernel(ckv_ref, kr_ref, gkv_ref, gkr_ref, cos_ref, sin_ref, ckv_o, kr_o, kcat_o,
               *, rank, rope_d):
    c = _rms(ckv_ref[...], gkv_ref[...])
    r = _rope_rows(_rms(kr_ref[:, :rope_d], gkr_ref[...]), cos_ref[...], sin_ref[...])
    ckv_o[...] = c
    kr_o[...] = r
    kcat_o[:, :rank] = c.astype(kcat_o.dtype)
    kcat_o[:, rank:] = r.astype(kcat_o.dtype)


def _mla_kv(z, g_kv, g_kr, cosf, sinf, *, layer, ckv_off, kr_off, tm):
    m = z.shape[0]
    rank = g_kv.shape[-1]
    rope_d = g_kr.shape[-1]
    assert ckv_off % rank == 0 and kr_off % LANES == 0
    return pl.pallas_call(
        functools.partial(_kv_kernel, rank=rank, rope_d=rope_d),
        grid=(m // tm,),
        in_specs=[
            pl.BlockSpec((tm, rank), lambda i: (i, ckv_off // rank)),
            pl.BlockSpec((tm, LANES), lambda i: (i, kr_off // LANES)),
            pl.BlockSpec((None, 1, rank), lambda i: (layer, 0, 0)),
            pl.BlockSpec((None, 1, rope_d), lambda i: (layer, 0, 0)),
            pl.BlockSpec((tm, rope_d), lambda i: (i, 0)),
            pl.BlockSpec((tm, rope_d), lambda i: (i, 0)),
        ],
        out_specs=[
            pl.BlockSpec((tm, rank), lambda i: (i, 0)),
            pl.BlockSpec((tm, rope_d), lambda i: (i, 0)),
            pl.BlockSpec((tm, rank + rope_d), lambda i: (i, 0)),
        ],
        out_shape=[
            jax.ShapeDtypeStruct((m, rank), F32),
            jax.ShapeDtypeStruct((m, rope_d), F32),
            jax.ShapeDtypeStruct((m, rank + rope_d), BF16),
        ],
        compiler_params=_cparams("parallel"),
        name="mla_kv",
    )(z, z, g_kv, g_kr, cosf, sinf)


def _pattn_kernel(q_ref, k_ref, wuv_ref, prev_ref, o_ref, m_sc, l_sc, acc_sc, *, heads, tq, rank, vd):
    del prev_ref
    qi = pl.program_id(2)

    m_sc[...] = jnp.full_like(m_sc, -jnp.inf)
    l_sc[...] = jnp.zeros_like(l_sc)
    acc_sc[...] = jnp.zeros_like(acc_sc)

    def step(kb, masked):
        kblk = k_ref[pl.ds(pl.multiple_of(kb * tq, tq), tq), :]
        half = tq // 2
        if masked:
            items = [(h, r0, r1, r1) for h in range(heads) for (r0, r1) in ((0, half), (half, tq))]
        else:
            items = [(h, 0, tq, tq) for h in range(heads)]

        def scores(i, _):
            h, r0, r1, nkeys = items[i]
            s = _dot_nt(q_ref[h, r0:r1, :], kblk[:nkeys])
            if masked:
                row = lax.broadcasted_iota(jnp.int32, s.shape, 0) + r0
                s = jnp.where(lax.broadcasted_iota(jnp.int32, s.shape, 1) <= row, s, -jnp.inf)
            return s

        def softmax(i, s):
            h, r0, r1, _ = items[i]
            rs = slice(h * tq + r0, h * tq + r1)
            m_prev = m_sc[rs, :]
            m_new = jnp.maximum(m_prev, jnp.max(s, axis=-1, keepdims=True))
            alpha = jnp.exp(m_prev - m_new)
            p = jnp.exp(s - m_new)
            l_sc[rs, :] = alpha * l_sc[rs, :] + jnp.sum(p, axis=-1, keepdims=True)
            m_sc[rs, :] = m_new
            return alpha, p.astype(BF16)

        def values(i, ap):
            h, r0, r1, nkeys = items[i]
            rs = slice(h * tq + r0, h * tq + r1)
            acc_sc[rs, :] = ap[0] * acc_sc[rs, :] + _dot(ap[1], kblk[:nkeys, :rank])

        _skewed(len(items), [scores, softmax, values])

    def body(kb, carry):
        step(kb, False)
        return carry

    lax.fori_loop(0, qi, body, 0)
    step(qi, True)
    for h in range(heads):
        rs = slice(h * tq, (h + 1) * tq)
        o_lat = (acc_sc[rs, :] / l_sc[rs, :]).astype(BF16)
        o_ref[:, h * vd:(h + 1) * vd] = _dot(o_lat, wuv_ref[h].astype(BF16)).astype(o_ref.dtype)


def _prompt_attention(qcat, kcat, w_uv, *, layer, bp, tp, tq):
    heads, m, qd = qcat.shape
    rank, vd = w_uv.shape[2], w_uv.shape[3]
    nq = tp // tq
    hg = _div_tile(heads, ATTN_HEADS_PER_STEP, 1)
    return pl.pallas_call(
        functools.partial(_pattn_kernel, heads=hg, tq=tq, rank=rank, vd=vd),
        grid=(bp, heads // hg, nq),
        in_specs=[
            pl.BlockSpec((hg, tq, qd), lambda b, g, qi: (g, b * nq + qi, 0)),
            pl.BlockSpec((tp, qd), lambda b, g, qi: (b, 0)),
            pl.BlockSpec((None, hg, rank, vd), lambda b, g, qi: (layer, g, 0, 0)),
            pl.BlockSpec(memory_space=pl.ANY),
        ],
        out_specs=pl.BlockSpec((tq, hg * vd), lambda b, g, qi: (b * nq + qi, g)),
        out_shape=jax.ShapeDtypeStruct((m, heads * vd), BF16),
        input_output_aliases={3: 0},
        scratch_shapes=[pltpu.VMEM((hg * tq, 1), F32), pltpu.VMEM((hg * tq, 1), F32),
                        pltpu.VMEM((hg * tq, rank), F32)],
        compiler_params=_cparams("parallel", "parallel", "arbitrary"),
        name="prompt_attn",
    )(qcat, kcat, w_uv, jnp.zeros((m, heads * vd), BF16))


def _sattn_kernel(pt_ref, q_ref, knew_ref, ckv_hbm, krt_hbm, o_ref, ckv_ring, krt_ring, kbuf, krt_buf, s_sc, sems,
                  m_sc, l_sc, acc_sc, *, layer, pb, page, rank, nchunks, total, t_new):
    b = pl.program_id(0)
    slots = ATTN_RING_SLOTS
    ahead = slots - 1

    def chunk_copies(page_ids, slot):
        copies = []
        for i, pid in enumerate(page_ids):
            copies.append(pltpu.make_async_copy(
                ckv_hbm.at[layer, pid], ckv_ring.at[slot, pl.ds(i * page, page), :], sems.at[slot]))
            copies.append(pltpu.make_async_copy(krt_hbm.at[layer, pid], krt_ring.at[slot, i], sems.at[slot]))
        return copies

    def start_chunk(g, slot):
        src = lax.rem(g, total)
        for cp in chunk_copies([pt_ref[src * pb + i] for i in range(pb)], slot):
            cp.start()

    def wait_chunk(slot):
        for cp in chunk_copies([0] * pb, slot):
            cp.wait()

    slot_of = lambda c: c % slots

    @pl.when(b == 0)
    def _prime():
        for g0 in range(ahead):
            start_chunk(g0, slot_of(g0))

    m_sc[...] = jnp.full_like(m_sc, -jnp.inf)
    l_sc[...] = jnp.zeros_like(l_sc)
    acc_sc[...] = jnp.zeros_like(acc_sc)
    q = q_ref[...]

    def online_update(s, values):
        m_prev = m_sc[...]
        m_new = jnp.maximum(m_prev, jnp.max(s, axis=-1, keepdims=True))
        alpha = jnp.exp(m_prev - m_new)
        p = jnp.exp(s - m_new)
        l_sc[...] = alpha * l_sc[...] + jnp.sum(p, axis=-1, keepdims=True)
        acc_sc[...] = alpha * acc_sc[...] + _dot(p.astype(BF16), values)
        m_sc[...] = m_new

    def stage_scores(c):
        slot, par = slot_of(c), c % 2
        wait_chunk(slot)
        start_chunk(b * nchunks + c + ahead, slot_of(c + ahead))
        kbuf[par] = ckv_ring[slot].astype(BF16)
        for i in range(pb):
            krt_buf[par, :, i * page:(i + 1) * page] = krt_ring[slot, i].astype(BF16)
        s_sc[par] = _dot_nt(q[:, :rank], kbuf[par]) + _dot(q[:, rank:], krt_buf[par])

    def consume(c):
        online_update(s_sc[c % 2], kbuf[c % 2])

    stage_scores(0)
    for c in range(nchunks):
        if c + 1 < nchunks:
            stage_scores(c + 1)
        consume(c)

    k_new = knew_ref[...]
    s_new = _dot_nt(q, k_new)
    row = lax.rem(lax.broadcasted_iota(jnp.int32, s_new.shape, 0), t_new)
    col = lax.broadcasted_iota(jnp.int32, s_new.shape, 1)
    online_update(jnp.where(col <= row, s_new, -jnp.inf), k_new[:, :rank])
    o_ref[...] = (acc_sc[...] / l_sc[...]).astype(o_ref.dtype)

    @pl.when(b == pl.num_programs(0) - 1)
    def _drain():
        for g1 in range(ahead):
            wait_chunk(slot_of(total + g1))


def _sample_attention(q_s, k_new, cache_ckv, cache_krope_t, page_table, *, layer, t_new):
    b, rows, qd = q_s.shape
    page, rank = cache_ckv.shape[2], cache_ckv.shape[3]
    rope_d = cache_krope_t.shape[2]
    n_pages = page_table.shape[1]
    pb = _div_tile(n_pages, PAGES_PER_STEP, 1)
    nchunks = n_pages // pb
    total = b * nchunks
    assert nchunks % ATTN_RING_SLOTS == 0
    grid_spec = pltpu.PrefetchScalarGridSpec(
        num_scalar_prefetch=1,
        grid=(b,),
        in_specs=[
            pl.BlockSpec((None, rows, qd), lambda bi, pt: (bi, 0, 0)),
            pl.BlockSpec((None, k_new.shape[1], qd), lambda bi, pt: (bi, 0, 0)),
            pl.BlockSpec(memory_space=pl.ANY),
            pl.BlockSpec(memory_space=pl.ANY),
        ],
        out_specs=pl.BlockSpec((None, rows, rank), lambda bi, pt: (bi, 0, 0)),
        scratch_shapes=[pltpu.VMEM((ATTN_RING_SLOTS, pb * page, rank), cache_ckv.dtype),
                        pltpu.VMEM((ATTN_RING_SLOTS, pb, rope_d, page), cache_krope_t.dtype),
                        pltpu.VMEM((2, pb * page, rank), BF16), pltpu.VMEM((2, rope_d, pb * page), BF16),
                        pltpu.VMEM((2, rows, pb * page), F32),
                        pltpu.SemaphoreType.DMA((ATTN_RING_SLOTS,)),
                        pltpu.VMEM((rows, 1), F32), pltpu.VMEM((rows, 1), F32),
                        pltpu.VMEM((rows, rank), F32)],
    )
    return pl.pallas_call(
        functools.partial(_sattn_kernel, layer=layer, pb=pb, page=page, rank=rank, nchunks=nchunks,
                          total=total, t_new=t_new),
        grid_spec=grid_spec,
        out_shape=jax.ShapeDtypeStruct((b, rows, rank), BF16),
        compiler_params=_cparams("arbitrary"),
        name="sample_attn",
    )(page_table.reshape(-1), q_s, k_new, cache_ckv, cache_krope_t)


def _uv_kernel(x_ref, w_ref, prev_ref, o_ref):
    del prev_ref
    o_ref[...] = _dot(x_ref[...], w_ref[...].astype(BF16)).astype(o_ref.dtype)


def _value_up(o_lat, w_uv, o_prev, *, layer, row0):
    heads, ms, rank = o_lat.shape
    vd = w_uv.shape[3]
    assert row0 % ms == 0
    return pl.pallas_call(
        _uv_kernel,
        grid=(heads,),
        in_specs=[
            pl.BlockSpec((None, ms, rank), lambda h: (h, 0, 0)),
            pl.BlockSpec((None, None, rank, vd), lambda h: (layer, h, 0, 0)),
            pl.BlockSpec(memory_space=pl.ANY),
        ],
        out_specs=pl.BlockSpec((ms, vd), lambda h: (row0 // ms, h)),
        out_shape=jax.ShapeDtypeStruct(o_prev.shape, o_prev.dtype),
        input_output_aliases={2: 0},
        compiler_params=_cparams("parallel"),
        name="value_up",
    )(o_lat, w_uv, o_prev)


def _mlstm_kernel(*refs, heads, dk, dv, rows, sub_rows, nsub, aliased):
    (q_ref, k_ref, v_ref, o_ref, gc_ref, gr_ref, bgc_ref, bgr_ref, gmh_ref, c0_ref, n0_ref, m0_ref) = refs[:12]
    h_ref, c_ref, n_ref, m_ref = refs[-4:]
    sub = pl.program_id(2)

    @pl.when(pl.program_id(1) == 0)
    def _load_state():
        c_ref[...] = c0_ref[...]
        n_ref[...] = n0_ref[...]
        m_ref[...] = m0_ref[...]

    if nsub > 1:
        @pl.when(sub == 0)
        def _clear_rows():
            h_ref[...] = jnp.zeros_like(h_ref)

    n_all = n_ref[...]
    m_all = m_ref[...]
    n_out = n_all
    m_out = m_all
    head_id = lax.broadcasted_iota(jnp.int32, (heads, 1), 0)

    g_col = GATE_CAP * jnp.tanh((gc_ref[:, :2 * heads] + bgc_ref[...]) / GATE_CAP)
    g_row = GATE_CAP * jnp.tanh((gr_ref[...] + bgr_ref[...]) / GATE_CAP)

    def log_sigmoid(x):
        return jnp.minimum(x, 0.0) - jnp.log(1.0 + jnp.exp(-jnp.abs(x)))

    li = lax.broadcasted_iota(jnp.int32, (rows, rows), 0)
    si = lax.broadcasted_iota(jnp.int32, (rows, rows), 1)
    causal = si <= li
    ri = lax.broadcasted_iota(jnp.int32, (rows, 1), 0)
    ci = lax.broadcasted_iota(jnp.int32, (1, rows), 1)
    lo, hi = sub * sub_rows, (sub + 1) * sub_rows
    row_ok = (ri >= lo) & (ri < hi)
    col_ok = (ci >= lo) & (ci < hi)
    k_scale = dk ** -0.5

    state_rows = []

    def gates(h, _):
        ig_c = jnp.where(row_ok, g_col[:, h:h + 1], -jnp.inf)
        lf_c = jnp.where(row_ok, log_sigmoid(g_col[:, heads + h:heads + h + 1]), 0.0)
        ig_r = jnp.where(col_ok, g_row[h:h + 1, :], -jnp.inf)
        lf_r = jnp.where(col_ok, log_sigmoid(g_row[heads + h:heads + h + 1, :]), 0.0)
        b_c = jnp.sum(jnp.where(causal, lf_r, 0.0), axis=1, keepdims=True)
        b_r = jnp.sum(jnp.where(li <= si, lf_c, 0.0), axis=0, keepdims=True)
        b_last = jnp.sum(lf_r, axis=1, keepdims=True)
        m0 = m_all[h:h + 1, :]
        dmat = jnp.where(causal, b_c - b_r + ig_r, -jnp.inf)
        inter = b_c + m0
        m_pos = jnp.maximum(inter, jnp.max(dmat, axis=1, keepdims=True))
        g_r = b_last - b_r + ig_r
        m_new = jnp.maximum(b_last + m0, jnp.max(g_r, axis=1, keepdims=True))
        return dict(decay_w=jnp.exp(dmat - m_pos), a=jnp.exp(inter - m_pos), floor=jnp.exp(-m_pos),
                    m_new=m_new, decay=jnp.exp(b_last + m0 - m_new), wk_c=jnp.exp(b_last - b_c + ig_c - m_new))

    def products(h, st):
        qb = q_ref[:, h * dk:(h + 1) * dk].astype(BF16)
        kh = k_ref[:, h * dk:(h + 1) * dk] * k_scale
        st["qk"] = _dot_nt(qb, kh.astype(BF16))
        st["qc"] = _dot(qb, c_ref[h].astype(BF16))
        return st

    def update(h, st):
        qh = q_ref[:, h * dk:(h + 1) * dk]
        kh = k_ref[:, h * dk:(h + 1) * dk] * k_scale
        vb = v_ref[:, h * dv:(h + 1) * dv].astype(BF16)
        n0 = n_all[h:h + 1, :]
        w = st["decay_w"] * st["qk"]
        st["num"] = st["a"] * st["qc"] + _dot(w.astype(BF16), vb)
        st["nq"] = st["a"] * jnp.sum(qh * n0, axis=1, keepdims=True) + jnp.sum(w, axis=1, keepdims=True)
        kw = st["wk_c"] * kh
        c_ref[h] = st["decay"] * c_ref[h] + _dot_tn(kw.astype(BF16), vb)
        state_rows.append((st["decay"] * n0 + jnp.sum(kw, axis=0, keepdims=True), st["m_new"]))
        return st

    def emit(h, st):
        hs = st["num"] / jnp.maximum(jnp.abs(st["nq"]), st["floor"])
        hn = _rms(hs, gmh_ref[:, h * dv:(h + 1) * dv])
        out = (_sigmoid(o_ref[:, h * dv:(h + 1) * dv]) * hn).astype(h_ref.dtype)
        cols = slice(h * dv, (h + 1) * dv)
        if nsub == 1:
            h_ref[:, cols] = out
        else:
            h_ref[:, cols] = jnp.where(row_ok, out, h_ref[:, cols])

    _skewed(heads, [gates, products, update, emit])
    for h, (n_new, m_new) in enumerate(state_rows):
        n_out = jnp.where(head_id == h, n_new, n_out)
        m_out = jnp.where(head_id == h, m_new, m_out)
    n_ref[...] = n_out
    m_ref[...] = m_out


def _mlstm(z, g_rows, b_gates_c, b_gates_r, g_mh, c0, n0, m0, h_prev, *, layer, nb, nc, nsub,
           rows, sub_rows, row_blk0):
    m = z.shape[0]
    _, heads, dk, dv = c0.shape
    qw, vw = heads * dk, heads * dv
    assert (2 * qw) % vw == 0 and (2 * qw + 2 * vw) % LANES == 0
    g_blk = (2 * qw + 2 * vw) // LANES
    blk = lambda b, c, s: row_blk0 + b * nc + c
    seq = lambda b, c, s: b * nsub + s
    in_specs = [
        pl.BlockSpec((rows, qw), lambda b, c, s: (blk(b, c, s), 0)),
        pl.BlockSpec((rows, qw), lambda b, c, s: (blk(b, c, s), 1)),
        pl.BlockSpec((rows, vw), lambda b, c, s: (blk(b, c, s), 2 * qw // vw)),
        pl.BlockSpec((rows, vw), lambda b, c, s: (blk(b, c, s), 2 * qw // vw + 1)),
        pl.BlockSpec((rows, LANES), lambda b, c, s: (blk(b, c, s), g_blk)),
        pl.BlockSpec((None, 2 * heads, rows), lambda b, c, s: (b * nc + c, 0, 0)),
        pl.BlockSpec((None, 1, 2 * heads), lambda b, c, s: (layer, 0, 0)),
        pl.BlockSpec((None, 2 * heads, 1), lambda b, c, s: (layer, 0, 0)),
        pl.BlockSpec((None, 1, vw), lambda b, c, s: (layer, 0, 0)),
        pl.BlockSpec((None, heads, dk, dv), lambda b, c, s: (seq(b, c, s), 0, 0, 0)),
        pl.BlockSpec((None, heads, dk), lambda b, c, s: (seq(b, c, s), 0, 0)),
        pl.BlockSpec((None, heads, 1), lambda b, c, s: (seq(b, c, s), 0, 0)),
    ]
    args = [z, z, z, z, z, g_rows, b_gates_c, b_gates_r, g_mh, c0, n0, m0]
    aliases = {}
    if h_prev is not None:
        in_specs.append(pl.BlockSpec(memory_space=pl.ANY))
        args.append(h_prev)
        aliases = {12: 0}
    return pl.pallas_call(
        functools.partial(_mlstm_kernel, heads=heads, dk=dk, dv=dv, rows=rows, sub_rows=sub_rows,
                          nsub=nsub, aliased=h_prev is not None),
        grid=(nb, nc, nsub),
        in_specs=in_specs,
        out_specs=[
            pl.BlockSpec((rows, vw), lambda b, c, s: (blk(b, c, s), 0)),
            pl.BlockSpec((None, heads, dk, dv), lambda b, c, s: (seq(b, c, s), 0, 0, 0)),
            pl.BlockSpec((None, heads, dk), lambda b, c, s: (seq(b, c, s), 0, 0)),
            pl.BlockSpec((None, heads, 1), lambda b, c, s: (seq(b, c, s), 0, 0)),
        ],
        out_shape=[
            jax.ShapeDtypeStruct((m, vw), BF16),
            jax.ShapeDtypeStruct((nb * nsub, heads, dk, dv), F32),
            jax.ShapeDtypeStruct((nb * nsub, heads, dk), F32),
            jax.ShapeDtypeStruct((nb * nsub, heads, 1), F32),
        ],
        input_output_aliases=aliases,
        compiler_params=_cparams("parallel", "arbitrary", "arbitrary"),
        name="mlstm",
    )(*args)


def _rope_tables(pos, rope_d):
    half = rope_d // 2
    inv = ROPE_THETA ** (-jnp.arange(half, dtype=F32) * (2.0 / rope_d))
    ang = pos.astype(F32)[:, None] * inv[None, :]
    cos, sin = jnp.cos(ang), jnp.sin(ang)
    return jnp.concatenate([cos, cos], axis=-1), jnp.concatenate([-sin, sin], axis=-1)


def kernel(x_prompt, x_sample, c_prompt, c_sample, cache_ckv, cache_krope, page_table,
           state_mlstm_C, state_mlstm_n, state_mlstm_m,
           norm_mix_g, ada_mix_w, ada_mix_b, norm_ffn_g, ada_ffn_w, ada_ffn_b, ffn_w_in, ffn_w_out,
           ev_w_in, ev_ln_v_g, ev_ln_v_b, ev_w_s, ev_b_s, ev_g_qa, ev_w_uq, ev_g_qk, ev_g_kv, ev_g_kr,
           ev_w_uk, ev_w_uv, ev_w_out, od_w_in, od_b_gates, od_g_mh, od_w_out):
    bp, tp, d = x_prompt.shape
    bs, ts, _ = x_sample.shape
    depth = norm_mix_g.shape[0]
    mp, ms = bp * tp, bs * ts
    m = mp + ms
    past = page_table.shape[1] * cache_ckv.shape[2]

    cond_rows = ms + 16
    assert bp <= SUBLANES
    c_all = jnp.concatenate([jnp.repeat(c_sample, ts, axis=0), c_prompt,
                             jnp.zeros((cond_rows - ms - bp, d), F32)], axis=0)

    def ada(w, b, l):
        return _matmul([c_all], w, layer=l, n_out=3 * d, tm=cond_rows, tn=_div_tile(3 * d, 1024, LANES),
                       tk=_div_tile(d, 2048, LANES), out_dtype=F32, bias=b.reshape(depth, 1, 3 * d),
                       silu_x=True, name="ada")

    tm = _div_tile(m, MM_TM_PREF, 16)
    tr = _div_tile(ms, 512, 16)
    assert tp % tr == 0

    def residual(xs_in, w, l, x, mod, name, split_out=False):
        tk, k_rem = _k_tile(xs_in[0].shape[1], 1536)
        resid = dict(res_p=x[0], res_s=x[1], res_s_row0=x[2], mod=mod, gate_col=2 * d,
                     bp=bp, tp=tp, split_out=split_out)
        out = _matmul(xs_in, w, layer=l, n_out=d, tm=tm, tn=_div_tile(d, 512, LANES), tk=tk, k_rem=k_rem,
                      out_dtype=F32, resid=resid, name=name)
        return (out[0], out[1], 0) if split_out else (out, out, mp)

    gw = ev_ln_v_g.shape[-1]
    groups, gm_chunk = ev_w_s.shape[1], ev_w_s.shape[2]
    qr_rank = ev_g_qa.shape[-1]
    heads = ev_w_uk.shape[1]
    nope, rank = ev_w_uk.shape[2], ev_w_uk.shape[3]
    rope_d = ev_g_kr.shape[-1]
    qd = rank + rope_d
    ev_in = ev_w_in.shape[-1]
    cq_off, ckv_off, kr_off = 2 * gw, 2 * gw + qr_rank, 2 * gw + qr_rank + rank
    mla_scale = float(nope + rope_d) ** -0.5
    pos = jnp.concatenate([jnp.tile(jnp.arange(tp), bp), jnp.tile(past + jnp.arange(ts), bs)])
    cosf, sinf = _rope_tables(pos, rope_d)
    ev_w_in_t = jnp.swapaxes(ev_w_in, 1, 2)
    od_w_in_t = jnp.swapaxes(od_w_in, 1, 2)
    cache_krope_t = jnp.swapaxes(cache_krope, 2, 3)

    def mix_matrix(w_s, b_s, t):
        lc = min(t, gm_chunk)
        reps = gm_chunk // lc
        w = jnp.where(jnp.tril(jnp.ones((lc, lc), dtype=bool)), w_s[:, :lc, :lc], 0.0)
        w_blk = jnp.einsum("ab,gts->gatbs", jnp.eye(reps, dtype=F32), w).reshape(groups, gm_chunk, gm_chunk)
        b_blk = jnp.tile(b_s[:, :lc], (1, reps)).T
        return w_blk, b_blk

    ml_heads, ml_dk, ml_dv = state_mlstm_C.shape[2], state_mlstm_C.shape[3], state_mlstm_C.shape[4]
    lp = min(tp, ML_CHUNK)
    ncp = tp // lp
    ml_sub = ML_SAMPLE_ROWS // ts
    assert ML_SAMPLE_ROWS % ts == 0 and bs % ml_sub == 0 and mp % ML_SAMPLE_ROWS == 0

    x = (x_prompt.reshape(mp, d), x_sample.reshape(ms, d), 0)
    ckv_p, kr_p, ckv_s, kr_s, v_s = [], [], [], [], []
    cp_l, np_l, mp_l, cs_l, ns_l, ms_l = [], [], [], [], [], []
    for l in range(depth):
        j = l // 2
        mod = ada(ada_mix_w, ada_mix_b, l)
        h = _modulate(*x, norm_mix_g.reshape(depth, 1, d), mod, ms=ms, layer=l, tr=tr, bp=bp, tp=tp)
        if l % 2 == 0:
            z = _matmul([h], ev_w_in_t, layer=j, n_out=ev_in, tm=tm, tn=min(ev_in, 1024),
                        tk=_div_tile(d, 1024, LANES), w_t=True, out_dtype=F32, name="ev_in")
            w_p, b_p = mix_matrix(ev_w_s[j], ev_b_s[j], tp)
            w_smp, b_smp = mix_matrix(ev_w_s[j], ev_b_s[j], ts)
            a, v_ln = _gmlp(z, ev_ln_v_g.reshape(-1, 1, gw), ev_ln_v_b.reshape(-1, 1, gw),
                            jnp.stack([w_p, w_smp]), jnp.stack([b_p, b_smp]),
                            layer=j, gw=gw, n_prompt_blocks=mp // gm_chunk)
            tmq = _div_tile(ms, 512, 16)
            w_uq_h = ev_w_uq[j].reshape(qr_rank, heads, nope + rope_d).transpose(1, 0, 2)
            qcat = _mla_q(z, ev_g_qa.reshape(-1, 1, qr_rank), w_uq_h, ev_g_qk.reshape(-1, 1, nope + rope_d),
                          cosf, sinf, ev_w_uk, layer=j, cq_off=cq_off, tm=tmq, hb=_div_tile(heads, 4, 1),
                          scale=mla_scale)
            ckv_n, kr_n, kcat = _mla_kv(z, ev_g_kv.reshape(-1, 1, rank), ev_g_kr.reshape(-1, 1, rope_d),
                                        cosf, sinf, layer=j, ckv_off=ckv_off, kr_off=kr_off, tm=tmq)
            o = _prompt_attention(qcat, kcat, ev_w_uv, layer=j, bp=bp, tp=tp, tq=_div_tile(tp, ATTN_TQ_PREF, 16))
            q_s = qcat[:, mp:].reshape(heads, bs, ts, qd).transpose(1, 0, 2, 3).reshape(bs, heads * ts, qd)
            k_new = jnp.pad(kcat[mp:].reshape(bs, ts, qd), ((0, 0), (0, LANES - ts), (0, 0)))
            o_lat_s = _sample_attention(q_s, k_new, cache_ckv, cache_krope_t, page_table, layer=j, t_new=ts)
            o_lat_s = o_lat_s.reshape(bs, heads, ts, rank).transpose(1, 0, 2, 3).reshape(heads, ms, rank)
            o = _value_up(o_lat_s, ev_w_uv, o, layer=j, row0=mp)
            x = residual([a, o], ev_w_out, j, x, mod, "ev_out")
            ckv_p.append(ckv_n[:mp].reshape(bp, tp, rank))
            kr_p.append(kr_n[:mp].reshape(bp, tp, rope_d))
            ckv_s.append(ckv_n[mp:].reshape(bs, ts, rank))
            kr_s.append(kr_n[mp:].reshape(bs, ts, rope_d))
            v_s.append(v_ln[mp:].reshape(bs, ts, gw))
        else:
            od_in = od_w_in.shape[-1]
            gate_off = od_in - 2 * ml_heads
            bg_c = od_b_gates.reshape(-1, 1, 2 * ml_heads)
            bg_r = od_b_gates.reshape(-1, 2 * ml_heads, 1)
            gmh = od_g_mh.reshape(-1, 1, ml_heads * ml_dv)
            z = _matmul([h], od_w_in_t, layer=j, n_out=od_in, tm=tm, tn=min(od_in, 1024),
                        tk=_div_tile(d, 1024, LANES), w_t=True, out_dtype=F32, name="od_in")
            gates = z[:, gate_off:]
            g_rows_p = jnp.swapaxes(gates[:mp].reshape(bp * ncp, lp, 2 * ml_heads), 1, 2)
            g_rows_s = jnp.swapaxes(gates[mp:].reshape(ms // ML_SAMPLE_ROWS, ML_SAMPLE_ROWS, 2 * ml_heads), 1, 2)
            zeros = lambda *shape: jnp.zeros(shape, F32)
            hm, c_p, n_p, m_p = _mlstm(z, g_rows_p, bg_c, bg_r, gmh, zeros(bp, ml_heads, ml_dk, ml_dv),
                                       zeros(bp, ml_heads, ml_dk), zeros(bp, ml_heads, 1),
                                       jnp.zeros((m, ml_heads * ml_dv), BF16),
                                       layer=j, nb=bp, nc=ncp, nsub=1, rows=lp, sub_rows=lp, row_blk0=0)
            hm, c_s, n_s, m_s = _mlstm(z, g_rows_s, bg_c, bg_r, gmh, state_mlstm_C[j], state_mlstm_n[j],
                                       state_mlstm_m[j].reshape(bs, ml_heads, 1), hm,
                                       layer=j, nb=bs // ml_sub, nc=1, nsub=ml_sub, rows=ML_SAMPLE_ROWS,
                                       sub_rows=ts, row_blk0=mp // ML_SAMPLE_ROWS)
            x = residual([hm], od_w_out, j, x, mod, "od_out")
            cp_l.append(c_p)
            np_l.append(n_p)
            mp_l.append(m_p.reshape(bp, ml_heads))
            cs_l.append(c_s)
            ns_l.append(n_s)
            ms_l.append(m_s.reshape(bs, ml_heads))

        mod = ada(ada_ffn_w, ada_ffn_b, l)
        h = _modulate(*x, norm_ffn_g.reshape(depth, 1, d), mod, ms=ms, layer=l, tr=tr, bp=bp, tp=tp)
        act = _swiglu_in(h, ffn_w_in, layer=l, tm=_div_tile(m, SWIGLU_TM_PREF, 16),
                         tk=_div_tile(d, 512, LANES), nsub=2)
        x = residual([act], ffn_w_out, l, x, mod, "ffn_out", split_out=l == depth - 1)

    return (x[0].reshape(bp, tp, d), x[1].reshape(bs, ts, d),
            jnp.stack(ckv_p), jnp.stack(kr_p), jnp.stack(ckv_s), jnp.stack(kr_s), jnp.stack(v_s),
            jnp.stack(cp_l), jnp.stack(np_l), jnp.stack(mp_l),
            jnp.stack(cs_l), jnp.stack(ns_l), jnp.stack(ms_l))
```

```python
import functools

import jax
import jax.numpy as jnp
from jax import lax
from jax.experimental import pallas as pl
from jax.experimental.pallas import tpu as pltpu

EPS = 1e-6
ROPE_THETA = 10000.0
GATE_CAP = 15.0
ML_CHUNK = 64
ML_SAMPLE_ROWS = 16
PAGES_PER_STEP = 16
ATTN_RING_SLOTS = 4
LANES = 128
SUBLANES = 8
V7X_VMEM_LIMIT = 56 * 1024 * 1024
MM_TM_PREF = 2176
SWIGLU_TM_PREF = 4352
ATTN_TQ_PREF = 512
ATTN_HEADS_PER_STEP = 4

F32 = jnp.float32
BF16 = jnp.bfloat16


def _cparams(*sem):
    return pltpu.CompilerParams(dimension_semantics=sem, vmem_limit_bytes=V7X_VMEM_LIMIT)


def _div_tile(dim, pref, align):
    if dim <= pref:
        return dim
    t = (pref // align) * align
    while t >= align:
        if dim % t == 0:
            return t
        t -= align
    return dim


def _k_tile(kdim, pref):
    exact = _div_tile(kdim, pref, 256)
    if kdim <= pref or 2 * exact > pref:
        return exact, 0
    for tk in range((pref // 256) * 256, exact, -256):
        rem = kdim % tk
        if rem and (kdim - rem) % rem == 0 and rem % LANES == 0:
            return tk, rem
    return exact, 0


def _rms(x, g):
    return x * lax.rsqrt(jnp.mean(x * x, axis=-1, keepdims=True) + EPS) * g


def _gelu(x):
    return x * (0.5 * (1.0 + jnp.tanh(0.7978845608028654 * (x + 0.044715 * (x * x * x)))))


def _sigmoid(x):
    return 0.5 * (jnp.tanh(0.5 * x) + 1.0)


def _dot(a, b):
    return jnp.dot(a, b, preferred_element_type=F32)


def _dot_nt(a, b):
    return lax.dot_general(a, b, (((1,), (1,)), ((), ())), preferred_element_type=F32)


def _dot_tn(a, b):
    return lax.dot_general(a, b, (((0,), (0,)), ((), ())), preferred_element_type=F32)


def _skewed(n, stages):
    vals = [None] * n
    for t in range(n + len(stages) - 1):
        for s, fn in enumerate(stages):
            i = t - s
            if 0 <= i < n:
                vals[i] = fn(i, vals[i])
    return vals


def _row_segments(block_start, tm, n_prompt_rows, tp):
    segs, r, end = [], block_start, block_start + tm
    while r < end:
        if r < n_prompt_rows:
            seq = r // tp
            hi = min(end, (seq + 1) * tp)
            segs.append((r - block_start, hi - block_start, "p", seq))
        else:
            hi = end
            segs.append((r - block_start, hi - block_start, "s", r - n_prompt_rows))
        r = hi
    return segs


def _mm_kernel(*refs, nk, nx, nka, w_t, has_rem, silu_x, has_bias, segments, split_out):
    it = iter(refs)
    x_refs = [next(it) for _ in range(nx)]
    w_ref = next(it)
    xr_ref = next(it) if has_rem else None
    wr_ref = next(it) if has_rem else None
    b_ref = next(it) if has_bias else None
    resp_ref = ress_ref = gp_ref = gs_ref = None
    if segments is not None:
        resp_ref, ress_ref, gp_ref, gs_ref = next(it), next(it), next(it), next(it)
    o_ref = next(it)
    os_ref = next(it) if split_out else None
    acc_ref = next(it)
    silu_sc = next(it) if silu_x else None
    k = pl.program_id(2)

    def accum(x_ref, first):
        if silu_x:
            @pl.when(pl.program_id(0) == 0)
            def _activate():
                x = x_ref[...]
                silu_sc[k] = (x * _sigmoid(x)).astype(BF16)
            xb = silu_sc[k]
        else:
            xb = x_ref[...].astype(BF16)
        w = w_ref[...].astype(BF16)
        prod = _dot_nt(xb, w) if w_t else _dot(xb, w)
        if not first:
            acc_ref[...] += prod
        elif has_rem:
            acc_ref[...] = prod + _dot(xr_ref[...].astype(BF16), wr_ref[...].astype(BF16))
        else:
            acc_ref[...] = prod

    pl.when(k == 0)(lambda: accum(x_refs[0], True))
    if nx == 1:
        pl.when(k > 0)(lambda: accum(x_refs[0], False))
    else:
        pl.when((k > 0) & (k < nka))(lambda: accum(x_refs[0], False))
        pl.when(k >= nka)(lambda: accum(x_refs[1], False))

    @pl.when(k == nk - 1)
    def _finish():
        if segments is None:
            y = acc_ref[...]
            if has_bias:
                y = y + b_ref[...]
            o_ref[...] = y.astype(o_ref.dtype)
        else:
            i = pl.program_id(1)
            for bi, segs in enumerate(segments):
                @pl.when(i == bi)
                def _gated(segs=segs):
                    for lo, hi, kind, idx in segs:
                        if kind == "p":
                            o_ref[lo:hi, :] = resp_ref[lo:hi, :] + gp_ref[idx:idx + 1, :] * acc_ref[lo:hi, :]
                        else:
                            srows = slice(idx, idx + hi - lo)
                            y = ress_ref[srows, :] + gs_ref[srows, :] * acc_ref[lo:hi, :]
                            if split_out:
                                os_ref[srows, :] = y
                            else:
                                o_ref[lo:hi, :] = y


def _matmul(xs, w, *, layer, n_out, tm, tn, tk, out_dtype, w_t=False, k_rem=0, bias=None,
            silu_x=False, resid=None, name="mm"):
    m, k_each = xs[0].shape
    nx = len(xs)
    assert m % tm == 0 and (k_each - k_rem) % tk == 0 and (nx == 1 or k_rem == 0)
    nka = (k_each - k_rem) // tk
    nk = nka * nx
    grid = (pl.cdiv(n_out, tn), m // tm, nk)

    def spec(shape, index):
        return pl.BlockSpec(shape, lambda j, i, k: index(i, j, k))

    in_specs = [spec((tm, tk), lambda i, j, k: (i, jnp.minimum(k, nka - 1)))]
    if nx == 2:
        in_specs.append(spec((tm, tk), lambda i, j, k: (i, jnp.maximum(k - nka, 0))))
    if w_t:
        in_specs.append(spec((None, tn, tk), lambda i, j, k: (layer, j, k)))
    else:
        in_specs.append(spec((None, tk, tn), lambda i, j, k: (layer, k, j)))
    args = list(xs) + [w]
    if k_rem:
        assert not w_t
        rem_blk = (k_each - k_rem) // k_rem
        in_specs.append(spec((tm, k_rem), lambda i, j, k: (i, rem_blk)))
        in_specs.append(spec((None, k_rem, tn), lambda i, j, k: (layer, rem_blk, j)))
        args += [xs[0], w]
    if bias is not None:
        in_specs.append(spec((None, 1, tn), lambda i, j, k: (layer, 0, j)))
        args.append(bias)
    segments = None
    split_out = False
    out_specs = spec((tm, tn), lambda i, j, k: (i, j))
    out_shape = jax.ShapeDtypeStruct((m, n_out), out_dtype)
    if resid is not None:
        n_prompt_rows, tp = resid["bp"] * resid["tp"], resid["tp"]
        gate_blk = resid["gate_col"] // tn
        assert resid["gate_col"] % tn == 0 and resid["bp"] <= SUBLANES
        segments = [_row_segments(b * tm, tm, n_prompt_rows, tp) for b in range(m // tm)]
        ms = m - n_prompt_rows
        res_s_blk = resid["res_s_row0"] // ms
        gate_p_blk = ms // SUBLANES
        assert resid["res_s_row0"] % ms == 0 and ms % SUBLANES == 0
        in_specs += [
            spec((tm, tn), lambda i, j, k: (i, j)),
            spec((ms, tn), lambda i, j, k: (res_s_blk, j)),
            spec((SUBLANES, tn), lambda i, j, k: (gate_p_blk, gate_blk + j)),
            spec((ms, tn), lambda i, j, k: (0, gate_blk + j)),
        ]
        args += [resid["res_p"], resid["res_s"], resid["mod"], resid["mod"]]
        split_out = resid["split_out"]
        if split_out:
            out_specs = [out_specs, spec((ms, tn), lambda i, j, k: (0, j))]
            out_shape = [jax.ShapeDtypeStruct((n_prompt_rows, n_out), out_dtype),
                         jax.ShapeDtypeStruct((ms, n_out), out_dtype)]
    kern = functools.partial(_mm_kernel, nk=nk, nx=nx, nka=nka, w_t=w_t, has_rem=bool(k_rem), silu_x=silu_x,
                             has_bias=bias is not None, segments=segments, split_out=split_out)
    return pl.pallas_call(
        kern,
        grid=grid,
        in_specs=in_specs,
        out_specs=out_specs,
        out_shape=out_shape,
        scratch_shapes=[pltpu.VMEM((tm, tn), F32)] + ([pltpu.VMEM((nk, tm, tk), BF16)] if silu_x else []),
        compiler_params=_cparams("arbitrary" if silu_x else "parallel", "arbitrary", "arbitrary"),
        name=name,
    )(*args)


SWIGLU_SUB = 256


def _swiglu_kernel(x_ref, *refs, nk, nsub):
    wg = refs[:nsub]
    wu = refs[nsub:2 * nsub]
    o_ref = refs[2 * nsub]
    accg, accu = refs[2 * nsub + 1:]
    k = pl.program_id(2)

    def accum(first):
        x = x_ref[...]
        for s in range(nsub):
            cols = slice(s * SWIGLU_SUB, (s + 1) * SWIGLU_SUB)
            pg = _dot(x, wg[s][...].astype(BF16))
            pu = _dot(x, wu[s][...].astype(BF16))
            if first:
                accg[:, cols] = pg
                accu[:, cols] = pu
            else:
                accg[:, cols] += pg
                accu[:, cols] += pu

    pl.when(k == 0)(lambda: accum(True))
    pl.when(k > 0)(lambda: accum(False))

    @pl.when(k == nk - 1)
    def _finish():
        g = accg[...]
        o_ref[...] = (g * _sigmoid(g) * accu[...]).astype(o_ref.dtype)


def _swiglu_in(x, w, *, layer, tm, tk, nsub):
    m, kdim = x.shape
    f = w.shape[2] // 2
    sub = SWIGLU_SUB
    assert f % sub == 0 and m % tm == 0 and kdim % tk == 0
    nfb = f // sub
    last = 2 * nfb - 1
    nk = kdim // tk
    tn = nsub * sub
    grid = (m // tm, pl.cdiv(f, tn), nk)
    in_specs = [pl.BlockSpec((tm, tk), lambda i, j, k: (i, k))]
    for s in range(nsub):
        in_specs.append(pl.BlockSpec((None, tk, sub), lambda i, j, k, s=s: (layer, k, j * nsub + s)))
    for s in range(nsub):
        in_specs.append(pl.BlockSpec(
            (None, tk, sub), lambda i, j, k, s=s: (layer, k, jnp.minimum(nfb + j * nsub + s, last))))
    return pl.pallas_call(
        functools.partial(_swiglu_kernel, nk=nk, nsub=nsub),
        grid=grid,
        in_specs=in_specs,
        out_specs=pl.BlockSpec((tm, tn), lambda i, j, k: (i, j)),
        out_shape=jax.ShapeDtypeStruct((m, f), BF16),
        scratch_shapes=[pltpu.VMEM((tm, tn), F32), pltpu.VMEM((tm, tn), F32)],
        compiler_params=_cparams("parallel", "parallel", "arbitrary"),
        name="swiglu_in",
    )(x, *([w] * (2 * nsub)))


def _modulate_kernel(xp_ref, xs_ref, g_ref, shp_ref, scp_ref, shs_ref, scs_ref, o_ref, *, tr, tp, bp, npb):
    i = pl.program_id(0)
    group = 16

    def sweep(x_ref, shift_of, scale_of):
        def body(r, carry):
            rows = pl.ds(pl.multiple_of(r * group, group), group)
            y = _rms(x_ref[rows, :], g_ref[...])
            o_ref[rows, :] = (y * (1.0 + scale_of(rows)) + shift_of(rows)).astype(o_ref.dtype)
            return carry
        lax.fori_loop(0, tr // group, body, 0, unroll=4)

    @pl.when(i < npb)
    def _prompt_rows():
        seq = jnp.minimum(i * tr // tp, bp - 1)
        shift = shp_ref[pl.ds(seq, 1), :]
        scale = scp_ref[pl.ds(seq, 1), :]
        sweep(xp_ref, lambda rows: shift, lambda rows: scale)

    @pl.when(i >= npb)
    def _sample_rows():
        sweep(xs_ref, lambda rows: shs_ref[rows, :], lambda rows: scs_ref[rows, :])


def _modulate(x_p, x_s, s_row0, g, mod, *, ms, layer, tr, bp, tp):
    d = x_p.shape[1]
    npb = bp * tp // tr
    m = bp * tp + ms
    assert s_row0 % tr == 0 and ms % SUBLANES == 0
    srow = lambda i: jnp.maximum(i - npb, 0)
    p_blk = ms // SUBLANES
    return pl.pallas_call(
        functools.partial(_modulate_kernel, tr=tr, tp=tp, bp=bp, npb=npb),
        grid=(m // tr,),
        in_specs=[
            pl.BlockSpec((tr, d), lambda i: (jnp.minimum(i, npb - 1), 0)),
            pl.BlockSpec((tr, d), lambda i: (s_row0 // tr + srow(i), 0), pipeline_mode=pl.Buffered(1)),
            pl.BlockSpec((None, 1, d), lambda i: (layer, 0, 0)),
            pl.BlockSpec((SUBLANES, d), lambda i: (p_blk, 0)),
            pl.BlockSpec((SUBLANES, d), lambda i: (p_blk, 1)),
            pl.BlockSpec((tr, d), lambda i: (srow(i), 0), pipeline_mode=pl.Buffered(1)),
            pl.BlockSpec((tr, d), lambda i: (srow(i), 1), pipeline_mode=pl.Buffered(1)),
        ],
        out_specs=pl.BlockSpec((tr, d), lambda i: (i, 0)),
        out_shape=jax.ShapeDtypeStruct((m, d), BF16),
        compiler_params=_cparams("parallel"),
        name="modulate",
    )(x_p, x_s, g, mod, mod, mod, mod)


def _gmlp_kernel(u_ref, v_ref, lng_ref, lnb_ref, w_ref, b_ref, a_ref, vout_ref, *, groups, gdim):
    for g in range(groups):
        cols = slice(g * gdim, (g + 1) * gdim)
        vg = _gelu(v_ref[:, cols])
        mu = jnp.mean(vg, axis=-1, keepdims=True)
        xc = vg - mu
        var = jnp.mean(xc * xc, axis=-1, keepdims=True)
        y = xc * lax.rsqrt(var + EPS) * lng_ref[:, cols] + lnb_ref[:, cols]
        vout_ref[:, cols] = y
        mixed = _dot(w_ref[g].astype(BF16), y.astype(BF16)) + b_ref[:, g:g + 1]
        a_ref[:, cols] = (_gelu(u_ref[:, cols]) * mixed).astype(a_ref.dtype)


def _gmlp(z, ln_g, ln_b, w_mix, b_mix, *, layer, gw, n_prompt_blocks):
    m = z.shape[0]
    groups, rows = w_mix.shape[1], w_mix.shape[2]
    assert m % rows == 0
    which = lambda i: jnp.where(i < n_prompt_blocks, 0, 1)
    return pl.pallas_call(
        functools.partial(_gmlp_kernel, groups=groups, gdim=gw // groups),
        grid=(m // rows,),
        in_specs=[
            pl.BlockSpec((rows, gw), lambda i: (i, 0)),
            pl.BlockSpec((rows, gw), lambda i: (i, 1)),
            pl.BlockSpec((None, 1, gw), lambda i: (layer, 0, 0)),
            pl.BlockSpec((None, 1, gw), lambda i: (layer, 0, 0)),
            pl.BlockSpec((None, groups, rows, rows), lambda i: (which(i), 0, 0, 0)),
            pl.BlockSpec((None, rows, groups), lambda i: (which(i), 0, 0)),
        ],
        out_specs=[pl.BlockSpec((rows, gw), lambda i: (i, 0)), pl.BlockSpec((rows, gw), lambda i: (i, 0))],
        out_shape=[jax.ShapeDtypeStruct((m, gw), BF16), jax.ShapeDtypeStruct((m, gw), F32)],
        compiler_params=_cparams("parallel"),
        name="gmlp",
    )(z, z, ln_g, ln_b, w_mix, b_mix)


def _rope_rows(x, cosf, sinf):
    half = x.shape[-1] // 2
    swapped = jnp.concatenate([x[:, half:], x[:, :half]], axis=-1)
    return x * cosf + swapped * sinf


def _q_kernel(cq_ref, gqa_ref, wuq_ref, gqk_ref, cos_ref, sin_ref, wuk_ref, o_ref, cqn_sc,
              *, hb, nope, rank, scale):
    @pl.when(pl.program_id(1) == 0)
    def _norm():
        cqn_sc[...] = _rms(cq_ref[...], gqa_ref[...]).astype(BF16)

    gain = gqk_ref[...] * scale

    def project(h, _):
        return _dot(cqn_sc[...], wuq_ref[h].astype(BF16))

    def normalise(h, q):
        q = _rms(q, gain)
        o_ref[h, :, rank:] = _rope_rows(q[:, nope:], cos_ref[...], sin_ref[...]).astype(o_ref.dtype)
        return q[:, :nope].astype(BF16)

    def absorb(h, q_nope):
        o_ref[h, :, :rank] = _dot(q_nope, wuk_ref[h].astype(BF16)).astype(o_ref.dtype)

    _skewed(hb, [project, normalise, absorb])


def _mla_q(z, g_qa, w_uq_h, g_qk, cosf, sinf, w_uk, *, layer, cq_off, tm, hb, scale):
    m = z.shape[0]
    qr = g_qa.shape[-1]
    heads, _, qd = w_uq_h.shape
    nope, rank = w_uk.shape[2], w_uk.shape[3]
    rope_d = qd - nope
    assert cq_off % qr == 0 and heads % hb == 0
    cq_blk = cq_off // qr
    return pl.pallas_call(
        functools.partial(_q_kernel, hb=hb, nope=nope, rank=rank, scale=scale),
        grid=(m // tm, heads // hb),
        in_specs=[
            pl.BlockSpec((tm, qr), lambda i, h: (i, cq_blk)),
            pl.BlockSpec((None, 1, qr), lambda i, h: (layer, 0, 0)),
            pl.BlockSpec((hb, qr, qd), lambda i, h: (h, 0, 0)),
            pl.BlockSpec((None, 1, qd), lambda i, h: (layer, 0, 0)),
            pl.BlockSpec((tm, rope_d), lambda i, h: (i, 0)),
            pl.BlockSpec((tm, rope_d), lambda i, h: (i, 0)),
            pl.BlockSpec((None, hb, nope, rank), lambda i, h: (layer, h, 0, 0)),
        ],
        out_specs=pl.BlockSpec((hb, tm, rank + rope_d), lambda i, h: (h, i, 0)),
        out_shape=jax.ShapeDtypeStruct((heads, m, rank + rope_d), BF16),
        scratch_shapes=[pltpu.VMEM((tm, qr), BF16)],
        compiler_params=_cparams("parallel", "arbitrary"),
        name="mla_q",
    )(z, g_qa, w_uq_h, g_qk, cosf, sinf, w_uk)


def _kv_kernel(ckv_ref, kr_ref, gkv_ref, gkr_ref, cos_ref, sin_ref, ckv_o, kr_o, kcat_o,
               *, rank, rope_d):
    c = _rms(ckv_ref[...], gkv_ref[...])
    r = _rope_rows(_rms(kr_ref[:, :rope_d], gkr_ref[...]), cos_ref[...], sin_ref[...])
    ckv_o[...] = c
    kr_o[...] = r
    kcat_o[:, :rank] = c.astype(kcat_o.dtype)
    kcat_o[:, rank:] = r.astype(kcat_o.dtype)


def _mla_kv(z, g_kv, g_kr, cosf, sinf, *, layer, ckv_off, kr_off, tm):
    m = z.shape[0]
    rank = g_kv.shape[-1]
    rope_d = g_kr.shape[-1]
    assert ckv_off % rank == 0 and kr_off % LANES == 0
    return pl.pallas_call(
        functools.partial(_kv_kernel, rank=rank, rope_d=rope_d),
        grid=(m // tm,),
        in_specs=[
            pl.BlockSpec((tm, rank), lambda i: (i, ckv_off // rank)),
            pl.BlockSpec((tm, LANES), lambda i: (i, kr_off // LANES)),
            pl.BlockSpec((None, 1, rank), lambda i: (layer, 0, 0)),
            pl.BlockSpec((None, 1, rope_d), lambda i: (layer, 0, 0)),
            pl.BlockSpec((tm, rope_d), lambda i: (i, 0)),
            pl.BlockSpec((tm, rope_d), lambda i: (i, 0)),
        ],
        out_specs=[
            pl.BlockSpec((tm, rank), lambda i: (i, 0)),
            pl.BlockSpec((tm, rope_d), lambda i: (i, 0)),
            pl.BlockSpec((tm, rank + rope_d), lambda i: (i, 0)),
        ],
        out_shape=[
            jax.ShapeDtypeStruct((m, rank), F32),
            jax.ShapeDtypeStruct((m, rope_d), F32),
            jax.ShapeDtypeStruct((m, rank + rope_d), BF16),
        ],
        compiler_params=_cparams("parallel"),
        name="mla_kv",
    )(z, z, g_kv, g_kr, cosf, sinf)


def _pattn_kernel(q_ref, k_ref, wuv_ref, prev_ref, o_ref, m_sc, l_sc, acc_sc, *, heads, tq, rank, vd):
    del prev_ref
    qi = pl.program_id(2)

    m_sc[...] = jnp.full_like(m_sc, -jnp.inf)
    l_sc[...] = jnp.zeros_like(l_sc)
    acc_sc[...] = jnp.zeros_like(acc_sc)

    def step(kb, masked):
        kblk = k_ref[pl.ds(pl.multiple_of(kb * tq, tq), tq), :]
        half = tq // 2
        if masked:
            items = [(h, r0, r1, r1) for h in range(heads) for (r0, r1) in ((0, half), (half, tq))]
        else:
            items = [(h, 0, tq, tq) for h in range(heads)]

        def scores(i, _):
            h, r0, r1, nkeys = items[i]
            s = _dot_nt(q_ref[h, r0:r1, :], kblk[:nkeys])
            if masked:
                row = lax.broadcasted_iota(jnp.int32, s.shape, 0) + r0
                s = jnp.where(lax.broadcasted_iota(jnp.int32, s.shape, 1) <= row, s, -jnp.inf)
            return s

        def softmax(i, s):
            h, r0, r1, _ = items[i]
            rs = slice(h * tq + r0, h * tq + r1)
            m_prev = m_sc[rs, :]
            m_new = jnp.maximum(m_prev, jnp.max(s, axis=-1, keepdims=True))
            alpha = jnp.exp(m_prev - m_new)
            p = jnp.exp(s - m_new)
            l_sc[rs, :] = alpha * l_sc[rs, :] + jnp.sum(p, axis=-1, keepdims=True)
            m_sc[rs, :] = m_new
            return alpha, p.astype(BF16)

        def values(i, ap):
            h, r0, r1, nkeys = items[i]
            rs = slice(h * tq + r0, h * tq + r1)
            acc_sc[rs, :] = ap[0] * acc_sc[rs, :] + _dot(ap[1], kblk[:nkeys, :rank])

        _skewed(len(items), [scores, softmax, values])

    def body(kb, carry):
        step(kb, False)
        return carry

    lax.fori_loop(0, qi, body, 0)
    step(qi, True)
    for h in range(heads):
        rs = slice(h * tq, (h + 1) * tq)
        o_lat = (acc_sc[rs, :] / l_sc[rs, :]).astype(BF16)
        o_ref[:, h * vd:(h + 1) * vd] = _dot(o_lat, wuv_ref[h].astype(BF16)).astype(o_ref.dtype)


def _prompt_attention(qcat, kcat, w_uv, *, layer, bp, tp, tq):
    heads, m, qd = qcat.shape
    rank, vd = w_uv.shape[2], w_uv.shape[3]
    nq = tp // tq
    hg = _div_tile(heads, ATTN_HEADS_PER_STEP, 1)
    return pl.pallas_call(
        functools.partial(_pattn_kernel, heads=hg, tq=tq, rank=rank, vd=vd),
        grid=(bp, heads // hg, nq),
        in_specs=[
            pl.BlockSpec((hg, tq, qd), lambda b, g, qi: (g, b * nq + qi, 0)),
            pl.BlockSpec((tp, qd), lambda b, g, qi: (b, 0)),
            pl.BlockSpec((None, hg, rank, vd), lambda b, g, qi: (layer, g, 0, 0)),
            pl.BlockSpec(memory_space=pl.ANY),
        ],
        out_specs=pl.BlockSpec((tq, hg * vd), lambda b, g, qi: (b * nq + qi, g)),
        out_shape=jax.ShapeDtypeStruct((m, heads * vd), BF16),
        input_output_aliases={3: 0},
        scratch_shapes=[pltpu.VMEM((hg * tq, 1), F32), pltpu.VMEM((hg * tq, 1), F32),
                        pltpu.VMEM((hg * tq, rank), F32)],
        compiler_params=_cparams("parallel", "parallel", "arbitrary"),
        name="prompt_attn",
    )(qcat, kcat, w_uv, jnp.zeros((m, heads * vd), BF16))


def _sattn_kernel(pt_ref, q_ref, knew_ref, ckv_hbm, krt_hbm, o_ref, ckv_ring, krt_ring, kbuf, krt_buf, s_sc, sems,
                  m_sc, l_sc, acc_sc, *, layer, pb, page, rank, nchunks, total, t_new):
    b = pl.program_id(0)
    slots = ATTN_RING_SLOTS
    ahead = slots - 1

    def chunk_copies(page_ids, slot):
        copies = []
        for i, pid in enumerate(page_ids):
            copies.append(pltpu.make_async_copy(
                ckv_hbm.at[layer, pid], ckv_ring.at[slot, pl.ds(i * page, page), :], sems.at[slot]))
            copies.append(pltpu.make_async_copy(krt_hbm.at[layer, pid], krt_ring.at[slot, i], sems.at[slot]))
        return copies

    def start_chunk(g, slot):
        src = lax.rem(g, total)
        for cp in chunk_copies([pt_ref[src * pb + i] for i in range(pb)], slot):
            cp.start()

    def wait_chunk(slot):
        for cp in chunk_copies([0] * pb, slot):
            cp.wait()

    slot_of = lambda c: c % slots

    @pl.when(b == 0)
    def _prime():
        for g0 in range(ahead):
            start_chunk(g0, slot_of(g0))

    m_sc[...] = jnp.full_like(m_sc, -jnp.inf)
    l_sc[...] = jnp.zeros_like(l_sc)
    acc_sc[...] = jnp.zeros_like(acc_sc)
    q = q_ref[...]

    def online_update(s, values):
        m_prev = m_sc[...]
        m_new = jnp.maximum(m_prev, jnp.max(s, axis=-1, keepdims=True))
        alpha = jnp.exp(m_prev - m_new)
        p = jnp.exp(s - m_new)
        l_sc[...] = alpha * l_sc[...] + jnp.sum(p, axis=-1, keepdims=True)
        acc_sc[...] = alpha * acc_sc[...] + _dot(p.astype(BF16), values)
        m_sc[...] = m_new

    def stage_scores(c):
        slot, par = slot_of(c), c % 2
        wait_chunk(slot)
        start_chunk(b * nchunks + c + ahead, slot_of(c + ahead))
        kbuf[par] = ckv_ring[slot].astype(BF16)
        for i in range(pb):
            krt_buf[par, :, i * page:(i + 1) * page] = krt_ring[slot, i].astype(BF16)
        s_sc[par] = _dot_nt(q[:, :rank], kbuf[par]) + _dot(q[:, rank:], krt_buf[par])

    def consume(c):
        online_update(s_sc[c % 2], kbuf[c % 2])

    stage_scores(0)
    for c in range(nchunks):
        if c + 1 < nchunks:
            stage_scores(c + 1)
        consume(c)

    k_new = knew_ref[...]
    s_new = _dot_nt(q, k_new)
    row = lax.rem(lax.broadcasted_iota(jnp.int32, s_new.shape, 0), t_new)
    col = lax.broadcasted_iota(jnp.int32, s_new.shape, 1)
    online_update(jnp.where(col <= row, s_new, -jnp.inf), k_new[:, :rank])
    o_ref[...] = (acc_sc[...] / l_sc[...]).astype(o_ref.dtype)

    @pl.when(b == pl.num_programs(0) - 1)
    def _drain():
        for g1 in range(ahead):
            wait_chunk(slot_of(total + g1))


def _sample_attention(q_s, k_new, cache_ckv, cache_krope_t, page_table, *, layer, t_new):
    b, rows, qd = q_s.shape
    page, rank = cache_ckv.shape[2], cache_ckv.shape[3]
    rope_d = cache_krope_t.shape[2]
    n_pages = page_table.shape[1]
    pb = _div_tile(n_pages, PAGES_PER_STEP, 1)
    nchunks = n_pages // pb
    total = b * nchunks
    assert nchunks % ATTN_RING_SLOTS == 0
    grid_spec = pltpu.PrefetchScalarGridSpec(
        num_scalar_prefetch=1,
        grid=(b,),
        in_specs=[
            pl.BlockSpec((None, rows, qd), lambda bi, pt: (bi, 0, 0)),
            pl.BlockSpec((None, k_new.shape[1], qd), lambda bi, pt: (bi, 0, 0)),
            pl.BlockSpec(memory_space=pl.ANY),
            pl.BlockSpec(memory_space=pl.ANY),
        ],
        out_specs=pl.BlockSpec((None, rows, rank), lambda bi, pt: (bi, 0, 0)),
        scratch_shapes=[pltpu.VMEM((ATTN_RING_SLOTS, pb * page, rank), cache_ckv.dtype),
                        pltpu.VMEM((ATTN_RING_SLOTS, pb, rope_d, page), cache_krope_t.dtype),
                        pltpu.VMEM((2, pb * page, rank), BF16), pltpu.VMEM((2, rope_d, pb * page), BF16),
                        pltpu.VMEM((2, rows, pb * page), F32),
                        pltpu.SemaphoreType.DMA((ATTN_RING_SLOTS,)),
                        pltpu.VMEM((rows, 1), F32), pltpu.VMEM((rows, 1), F32),
                        pltpu.VMEM((rows, rank), F32)],
    )
    return pl.pallas_call(
        functools.partial(_sattn_kernel, layer=layer, pb=pb, page=page, rank=rank, nchunks=nchunks,
                          total=total, t_new=t_new),
        grid_spec=grid_spec,
        out_shape=jax.ShapeDtypeStruct((b, rows, rank), BF16),
        compiler_params=_cparams("arbitrary"),
        name="sample_attn",
    )(page_table.reshape(-1), q_s, k_new, cache_ckv, cache_krope_t)


def _uv_kernel(x_ref, w_ref, prev_ref, o_ref):
    del prev_ref
    o_ref[...] = _dot(x_ref[...], w_ref[...].astype(BF16)).astype(o_ref.dtype)


def _value_up(o_lat, w_uv, o_prev, *, layer, row0):
    heads, ms, rank = o_lat.shape
    vd = w_uv.shape[3]
    assert row0 % ms == 0
    return pl.pallas_call(
        _uv_kernel,
        grid=(heads,),
        in_specs=[
            pl.BlockSpec((None, ms, rank), lambda h: (h, 0, 0)),
            pl.BlockSpec((None, None, rank, vd), lambda h: (layer, h, 0, 0)),
            pl.BlockSpec(memory_space=pl.ANY),
        ],
        out_specs=pl.BlockSpec((ms, vd), lambda h: (row0 // ms, h)),
        out_shape=jax.ShapeDtypeStruct(o_prev.shape, o_prev.dtype),
        input_output_aliases={2: 0},
        compiler_params=_cparams("parallel"),
        name="value_up",
    )(o_lat, w_uv, o_prev)


def _mlstm_kernel(*refs, heads, dk, dv, rows, sub_rows, nsub, aliased):
    (q_ref, k_ref, v_ref, o_ref, gc_ref, gr_ref, bgc_ref, bgr_ref, gmh_ref, c0_ref, n0_ref, m0_ref) = refs[:12]
    h_ref, c_ref, n_ref, m_ref = refs[-4:]
    sub = pl.program_id(2)

    @pl.when(pl.program_id(1) == 0)
    def _load_state():
        c_ref[...] = c0_ref[...]
        n_ref[...] = n0_ref[...]
        m_ref[...] = m0_ref[...]

    if nsub > 1:
        @pl.when(sub == 0)
        def _clear_rows():
            h_ref[...] = jnp.zeros_like(h_ref)

    n_all = n_ref[...]
    m_all = m_ref[...]
    n_out = n_all
    m_out = m_all
    head_id = lax.broadcasted_iota(jnp.int32, (heads, 1), 0)

    g_col = GATE_CAP * jnp.tanh((gc_ref[:, :2 * heads] + bgc_ref[...]) / GATE_CAP)
    g_row = GATE_CAP * jnp.tanh((gr_ref[...] + bgr_ref[...]) / GATE_CAP)

    def log_sigmoid(x):
        return jnp.minimum(x, 0.0) - jnp.log(1.0 + jnp.exp(-jnp.abs(x)))

    li = lax.broadcasted_iota(jnp.int32, (rows, rows), 0)
    si = lax.broadcasted_iota(jnp.int32, (rows, rows), 1)
    causal = si <= li
    ri = lax.broadcasted_iota(jnp.int32, (rows, 1), 0)
    ci = lax.broadcasted_iota(jnp.int32, (1, rows), 1)
    lo, hi = sub * sub_rows, (sub + 1) * sub_rows
    row_ok = (ri >= lo) & (ri < hi)
    col_ok = (ci >= lo) & (ci < hi)
    k_scale = dk ** -0.5

    state_rows = []

    def gates(h, _):
        ig_c = jnp.where(row_ok, g_col[:, h:h + 1], -jnp.inf)
        lf_c = jnp.where(row_ok, log_sigmoid(g_col[:, heads + h:heads + h + 1]), 0.0)
        ig_r = jnp.where(col_ok, g_row[h:h + 1, :], -jnp.inf)
        lf_r = jnp.where(col_ok, log_sigmoid(g_row[heads + h:heads + h + 1, :]), 0.0)
        b_c = jnp.sum(jnp.where(causal, lf_r, 0.0), axis=1, keepdims=True)
        b_r = jnp.sum(jnp.where(li <= si, lf_c, 0.0), axis=0, keepdims=True)
        b_last = jnp.sum(lf_r, axis=1, keepdims=True)
        m0 = m_all[h:h + 1, :]
        dmat = jnp.where(causal, b_c - b_r + ig_r, -jnp.inf)
        inter = b_c + m0
        m_pos = jnp.maximum(inter, jnp.max(dmat, axis=1, keepdims=True))
        g_r = b_last - b_r + ig_r
        m_new = jnp.maximum(b_last + m0, jnp.max(g_r, axis=1, keepdims=True))
        return dict(decay_w=jnp.exp(dmat - m_pos), a=jnp.exp(inter - m_pos), floor=jnp.exp(-m_pos),
                    m_new=m_new, decay=jnp.exp(b_last + m0 - m_new), wk_c=jnp.exp(b_last - b_c + ig_c - m_new))

    def products(h, st):
        qb = q_ref[:, h * dk:(h + 1) * dk].astype(BF16)
        kh = k_ref[:, h * dk:(h + 1) * dk] * k_scale
        st["qk"] = _dot_nt(qb, kh.astype(BF16))
        st["qc"] = _dot(qb, c_ref[h].astype(BF16))
        return st

    def update(h, st):
        qh = q_ref[:, h * dk:(h + 1) * dk]
        kh = k_ref[:, h * dk:(h + 1) * dk] * k_scale
        vb = v_ref[:, h * dv:(h + 1) * dv].astype(BF16)
        n0 = n_all[h:h + 1, :]
        w = st["decay_w"] * st["qk"]
        st["num"] = st["a"] * st["qc"] + _dot(w.astype(BF16), vb)
        st["nq"] = st["a"] * jnp.sum(qh * n0, axis=1, keepdims=True) + jnp.sum(w, axis=1, keepdims=True)
        kw = st["wk_c"] * kh
        c_ref[h] = st["decay"] * c_ref[h] + _dot_tn(kw.astype(BF16), vb)
        state_rows.append((st["decay"] * n0 + jnp.sum(kw, axis=0, keepdims=True), st["m_new"]))
        return st

    def emit(h, st):
        hs = st["num"] / jnp.maximum(jnp.abs(st["nq"]), st["floor"])
        hn = _rms(hs, gmh_ref[:, h * dv:(h + 1) * dv])
        out = (_sigmoid(o_ref[:, h * dv:(h + 1) * dv]) * hn).astype(h_ref.dtype)
        cols = slice(h * dv, (h + 1) * dv)
        if nsub == 1:
            h_ref[:, cols] = out
        else:
            h_ref[:, cols] = jnp.where(row_ok, out, h_ref[:, cols])

    _skewed(heads, [gates, products, update, emit])
    for h, (n_new, m_new) in enumerate(state_rows):
        n_out = jnp.where(head_id == h, n_new, n_out)
        m_out = jnp.where(head_id == h, m_new, m_out)
    n_ref[...] = n_out
    m_ref[...] = m_out


def _mlstm(z, g_rows, b_gates_c, b_gates_r, g_mh, c0, n0, m0, h_prev, *, layer, nb, nc, nsub,
           rows, sub_rows, row_blk0):
    m = z.shape[0]
    _, heads, dk, dv = c0.shape
    qw, vw = heads * dk, heads * dv
    assert (2 * qw) % vw == 0 and (2 * qw + 2 * vw) % LANES == 0
    g_blk = (2 * qw + 2 * vw) // LANES
    blk = lambda b, c, s: row_blk0 + b * nc + c
    seq = lambda b, c, s: b * nsub + s
    in_specs = [
        pl.BlockSpec((rows, qw), lambda b, c, s: (blk(b, c, s), 0)),
        pl.BlockSpec((rows, qw), lambda b, c, s: (blk(b, c, s), 1)),
        pl.BlockSpec((rows, vw), lambda b, c, s: (blk(b, c, s), 2 * qw // vw)),
        pl.BlockSpec((rows, vw), lambda b, c, s: (blk(b, c, s), 2 * qw // vw + 1)),
        pl.BlockSpec((rows, LANES), lambda b, c, s: (blk(b, c, s), g_blk)),
        pl.BlockSpec((None, 2 * heads, rows), lambda b, c, s: (b * nc + c, 0, 0)),
        pl.BlockSpec((None, 1, 2 * heads), lambda b, c, s: (layer, 0, 0)),
        pl.BlockSpec((None, 2 * heads, 1), lambda b, c, s: (layer, 0, 0)),
        pl.BlockSpec((None, 1, vw), lambda b, c, s: (layer, 0, 0)),
        pl.BlockSpec((None, heads, dk, dv), lambda b, c, s: (seq(b, c, s), 0, 0, 0)),
        pl.BlockSpec((None, heads, dk), lambda b, c, s: (seq(b, c, s), 0, 0)),
        pl.BlockSpec((None, heads, 1), lambda b, c, s: (seq(b, c, s), 0, 0)),
    ]
    args = [z, z, z, z, z, g_rows, b_gates_c, b_gates_r, g_mh, c0, n0, m0]
    aliases = {}
    if h_prev is not None:
        in_specs.append(pl.BlockSpec(memory_space=pl.ANY))
        args.append(h_prev)
        aliases = {12: 0}
    return pl.pallas_call(
        functools.partial(_mlstm_kernel, heads=heads, dk=dk, dv=dv, rows=rows, sub_rows=sub_rows,
                          nsub=nsub, aliased=h_prev is not None),
        grid=(nb, nc, nsub),
        in_specs=in_specs,
        out_specs=[
            pl.BlockSpec((rows, vw), lambda b, c, s: (blk(b, c, s), 0)),
            pl.BlockSpec((None, heads, dk, dv), lambda b, c, s: (seq(b, c, s), 0, 0, 0)),
            pl.BlockSpec((None, heads, dk), lambda b, c, s: (seq(b, c, s), 0, 0)),
            pl.BlockSpec((None, heads, 1), lambda b, c, s: (seq(b, c, s), 0, 0)),
        ],
        out_shape=[
            jax.ShapeDtypeStruct((m, vw), BF16),
            jax.ShapeDtypeStruct((nb * nsub, heads, dk, dv), F32),
            jax.ShapeDtypeStruct((nb * nsub, heads, dk), F32),
            jax.ShapeDtypeStruct((nb * nsub, heads, 1), F32),
        ],
        input_output_aliases=aliases,
        compiler_params=_cparams("parallel", "arbitrary", "arbitrary"),
        name="mlstm",
    )(*args)


def _rope_tables(pos, rope_d):
    half = rope_d // 2
    inv = ROPE_THETA ** (-jnp.arange(half, dtype=F32) * (2.0 / rope_d))
    ang = pos.astype(F32)[:, None] * inv[None, :]
    cos, sin = jnp.cos(ang), jnp.sin(ang)
    return jnp.concatenate([cos, cos], axis=-1), jnp.concatenate([-sin, sin], axis=-1)


def kernel(x_prompt, x_sample, c_prompt, c_sample, cache_ckv, cache_krope, page_table,
           state_mlstm_C, state_mlstm_n, state_mlstm_m,
           norm_mix_g, ada_mix_w, ada_mix_b, norm_ffn_g, ada_ffn_w, ada_ffn_b, ffn_w_in, ffn_w_out,
           ev_w_in, ev_ln_v_g, ev_ln_v_b, ev_w_s, ev_b_s, ev_g_qa, ev_w_uq, ev_g_qk, ev_g_kv, ev_g_kr,
           ev_w_uk, ev_w_uv, ev_w_out, od_w_in, od_b_gates, od_g_mh, od_w_out):
    bp, tp, d = x_prompt.shape
    bs, ts, _ = x_sample.shape
    depth = norm_mix_g.shape[0]
    mp, ms = bp * tp, bs * ts
    m = mp + ms
    past = page_table.shape[1] * cache_ckv.shape[2]

    cond_rows = ms + 16
    assert bp <= SUBLANES
    c_all = jnp.concatenate([jnp.repeat(c_sample, ts, axis=0), c_prompt,
                             jnp.zeros((cond_rows - ms - bp, d), F32)], axis=0)

    def ada(w, b, l):
        return _matmul([c_all], w, layer=l, n_out=3 * d, tm=cond_rows, tn=_div_tile(3 * d, 512, LANES),
                       tk=d, out_dtype=F32, bias=b.reshape(depth, 1, 3 * d),
                       silu_x=True, name="ada")

    tm = _div_tile(m, MM_TM_PREF, 16)
    tr = _div_tile(ms, 512, 16)
    assert tp % tr == 0

    def residual(xs_in, w, l, x, mod, name, split_out=False):
        tk, k_rem = _k_tile(xs_in[0].shape[1], 1536)
        resid = dict(res_p=x[0], res_s=x[1], res_s_row0=x[2], mod=mod, gate_col=2 * d,
                     bp=bp, tp=tp, split_out=split_out)
        out = _matmul(xs_in, w, layer=l, n_out=d, tm=tm, tn=_div_tile(d, 512, LANES), tk=tk, k_rem=k_rem,
                      out_dtype=F32, resid=resid, name=name)
        return (out[0], out[1], 0) if split_out else (out, out, mp)

    gw = ev_ln_v_g.shape[-1]
    groups, gm_chunk = ev_w_s.shape[1], ev_w_s.shape[2]
    qr_rank = ev_g_qa.shape[-1]
    heads = ev_w_uk.shape[1]
    nope, rank = ev_w_uk.shape[2], ev_w_uk.shape[3]
    rope_d = ev_g_kr.shape[-1]
    qd = rank + rope_d
    ev_in = ev_w_in.shape[-1]
    cq_off, ckv_off, kr_off = 2 * gw, 2 * gw + qr_rank, 2 * gw + qr_rank + rank
    mla_scale = float(nope + rope_d) ** -0.5
    pos = jnp.concatenate([jnp.tile(jnp.arange(tp), bp), jnp.tile(past + jnp.arange(ts), bs)])
    cosf, sinf = _rope_tables(pos, rope_d)
    ev_w_in_t = jnp.swapaxes(ev_w_in, 1, 2)
    od_w_in_t = jnp.swapaxes(od_w_in, 1, 2)
    cache_krope_t = jnp.swapaxes(cache_krope, 2, 3)

    def mix_matrix(w_s, b_s, t):
        lc = min(t, gm_chunk)
        reps = gm_chunk // lc
        w = jnp.where(jnp.tril(jnp.ones((lc, lc), dtype=bool)), w_s[:, :lc, :lc], 0.0)
        w_blk = jnp.einsum("ab,gts->gatbs", jnp.eye(reps, dtype=F32), w).reshape(groups, gm_chunk, gm_chunk)
        b_blk = jnp.tile(b_s[:, :lc], (1, reps)).T
        return w_blk, b_blk

    ml_heads, ml_dk, ml_dv = state_mlstm_C.shape[2], state_mlstm_C.shape[3], state_mlstm_C.shape[4]
    lp = min(tp, ML_CHUNK)
    ncp = tp // lp
    ml_sub = ML_SAMPLE_ROWS // ts
    assert ML_SAMPLE_ROWS % ts == 0 and bs % ml_sub == 0 and mp % ML_SAMPLE_ROWS == 0

    x = (x_prompt.reshape(mp, d), x_sample.reshape(ms, d), 0)
    ckv_p, kr_p, ckv_s, kr_s, v_s = [], [], [], [], []
    cp_l, np_l, mp_l, cs_l, ns_l, ms_l = [], [], [], [], [], []
    for l in range(depth):
        j = l // 2
        mod = ada(ada_mix_w, ada_mix_b, l)
        h = _modulate(*x, norm_mix_g.reshape(depth, 1, d), mod, ms=ms, layer=l, tr=tr, bp=bp, tp=tp)
        if l % 2 == 0:
            z = _matmul([h], ev_w_in_t, layer=j, n_out=ev_in, tm=tm, tn=min(ev_in, 1024),
                        tk=_div_tile(d, 1024, LANES), w_t=True, out_dtype=F32, name="ev_in")
            w_p, b_p = mix_matrix(ev_w_s[j], ev_b_s[j], tp)
            w_smp, b_smp = mix_matrix(ev_w_s[j], ev_b_s[j], ts)
            a, v_ln = _gmlp(z, ev_ln_v_g.reshape(-1, 1, gw), ev_ln_v_b.reshape(-1, 1, gw),
                            jnp.stack([w_p, w_smp]), jnp.stack([b_p, b_smp]),
                            layer=j, gw=gw, n_prompt_blocks=mp // gm_chunk)
            tmq = _div_tile(ms, 512, 16)
            w_uq_h = ev_w_uq[j].reshape(qr_rank, heads, nope + rope_d).transpose(1, 0, 2)
            qcat = _mla_q(z, ev_g_qa.reshape(-1, 1, qr_rank), w_uq_h, ev_g_qk.reshape(-1, 1, nope + rope_d),
                          cosf, sinf, ev_w_uk, layer=j, cq_off=cq_off, tm=tmq, hb=_div_tile(heads, 4, 1),
                          scale=mla_scale)
            ckv_n, kr_n, kcat = _mla_kv(z, ev_g_kv.reshape(-1, 1, rank), ev_g_kr.reshape(-1, 1, rope_d),
                                        cosf, sinf, layer=j, ckv_off=ckv_off, kr_off=kr_off, tm=tmq)
            o = _prompt_attention(qcat, kcat, ev_w_uv, layer=j, bp=bp, tp=tp, tq=_div_tile(tp, ATTN_TQ_PREF, 16))
            q_s = qcat[:, mp:].reshape(heads, bs, ts, qd).transpose(1, 0, 2, 3).reshape(bs, heads * ts, qd)
            k_new = jnp.pad(kcat[mp:].reshape(bs, ts, qd), ((0, 0), (0, LANES - ts), (0, 0)))
            o_lat_s = _sample_attention(q_s, k_new, cache_ckv, cache_krope_t, page_table, layer=j, t_new=ts)
            o_lat_s = o_lat_s.reshape(bs, heads, ts, rank).transpose(1, 0, 2, 3).reshape(heads, ms, rank)
            o = _value_up(o_lat_s, ev_w_uv, o, layer=j, row0=mp)
            x = residual([a, o], ev_w_out, j, x, mod, "ev_out")
            ckv_p.append(ckv_n[:mp].reshape(bp, tp, rank))
            kr_p.append(kr_n[:mp].reshape(bp, tp, rope_d))
            ckv_s.append(ckv_n[mp:].reshape(bs, ts, rank))
            kr_s.append(kr_n[mp:].reshape(bs, ts, rope_d))
            v_s.append(v_ln[mp:].reshape(bs, ts, gw))
        else:
            od_in = od_w_in.shape[-1]
            gate_off = od_in - 2 * ml_heads
            bg_c = od_b_gates.reshape(-1, 1, 2 * ml_heads)
            bg_r = od_b_gates.reshape(-1, 2 * ml_heads, 1)
            gmh = od_g_mh.reshape(-1, 1, ml_heads * ml_dv)
            z = _matmul([h], od_w_in_t, layer=j, n_out=od_in, tm=tm, tn=min(od_in, 1024),
                        tk=_div_tile(d, 1024, LANES), w_t=True, out_dtype=F32, name="od_in")
            gates = z[:, gate_off:]
            g_rows_p = jnp.swapaxes(gates[:mp].reshape(bp * ncp, lp, 2 * ml_heads), 1, 2)
            g_rows_s = jnp.swapaxes(gates[mp:].reshape(ms // ML_SAMPLE_ROWS, ML_SAMPLE_ROWS, 2 * ml_heads), 1, 2)
            zeros = lambda *shape: jnp.zeros(shape, F32)
            hm, c_p, n_p, m_p = _mlstm(z, g_rows_p, bg_c, bg_r, gmh, zeros(bp, ml_heads, ml_dk, ml_dv),
                                       zeros(bp, ml_heads, ml_dk), zeros(bp, ml_heads, 1),
                                       jnp.zeros((m, ml_heads * ml_dv), BF16),
                                       layer=j, nb=bp, nc=ncp, nsub=1, rows=lp, sub_rows=lp, row_blk0=0)
            hm, c_s, n_s, m_s = _mlstm(z, g_rows_s, bg_c, bg_r, gmh, state_mlstm_C[j], state_mlstm_n[j],
                                       state_mlstm_m[j].reshape(bs, ml_heads, 1), hm,
                                       layer=j, nb=bs // ml_sub, nc=1, nsub=ml_sub, rows=ML_SAMPLE_ROWS,
                                       sub_rows=ts, row_blk0=mp // ML_SAMPLE_ROWS)
            x = residual([hm], od_w_out, j, x, mod, "od_out")
            cp_l.append(c_p)
            np_l.append(n_p)
            mp_l.append(m_p.reshape(bp, ml_heads))
            cs_l.append(c_s)
            ns_l.append(n_s)
            ms_l.append(m_s.reshape(bs, ml_heads))

        mod = ada(ada_ffn_w, ada_ffn_b, l)
        h = _modulate(*x, norm_ffn_g.reshape(depth, 1, d), mod, ms=ms, layer=l, tr=tr, bp=bp, tp=tp)
        act = _swiglu_in(h, ffn_w_in, layer=l, tm=_div_tile(m, SWIGLU_TM_PREF, 16),
                         tk=_div_tile(d, 512, LANES), nsub=2)
        x = residual([act], ffn_w_out, l, x, mod, "ffn_out", split_out=l == depth - 1)

    return (x[0].reshape(bp, tp, d), x[1].reshape(bs, ts, d),
            jnp.stack(ckv_p), jnp.stack(kr_p), jnp.stack(ckv_s), jnp.stack(kr_s), jnp.stack(v_s),
            jnp.stack(cp_l), jnp.stack(np_l), jnp.stack(mp_l),
            jnp.stack(cs_l), jnp.stack(ns_l), jnp.stack(ms_l))
```
